```python
import math
import jax
import jax.numpy as jnp
from jax import lax
import numpy as np

D_MODEL = 1024
BATCH = 16
SEQ = 2048
DEPTH = 2
DEC_BATCH = 32
DEC_SEQ = 8
PAST_LEN = 16384
PAGE_SIZE = 128

HEAD_DIM = 64
D_FF = 4 * D_MODEL
RMS_EPS = 1e-6
A_BRANCHES = ((128, 1), (512, 4), (2048, 16))
A_HEADS = 8
A_W = len(A_BRANCHES) * A_HEADS * HEAD_DIM
A_OUT = A_HEADS * HEAD_DIM
B_HEADS = 4
B_DK = 64
B_DV = 128
B_GATE_RANK = 16
B_GATE_TEMP = 16.0
B_CHUNK = 64
B_OUT = B_HEADS * B_DV
C_HEADS = D_MODEL // HEAD_DIM
C_W = C_HEADS * HEAD_DIM
C_QBLOCK = 128
C_PAGES_PER_BLOCK = 16
C_FORGET_BIAS = 8.0
N_EVEN = (DEPTH + 1) // 2
N_ODD = DEPTH // 2
IN_AB_WIDTH = 3 * A_W + 2 * B_HEADS * B_DK + 2 * B_OUT + B_GATE_RANK
MIX_AB_WIDTH = A_OUT + B_OUT
IN_FOX_WIDTH = 3 * C_W + C_HEADS

kernel_name = 'hybrid_dilated_gla_fox_step'


def _split_points(sizes):
    pts, acc = [], 0
    for s in sizes[:-1]:
        acc += s
        pts.append(acc)
    return pts


def rmsnorm(x, g):
    xf = x.astype(jnp.float32)
    y = xf * lax.rsqrt(jnp.mean(xf * xf, axis=-1, keepdims=True) + RMS_EPS)
    return (y * g.astype(jnp.float32)).astype(x.dtype)


def adaln_terms(c, w, b):
    m = (c @ w + b)[:, None, :]
    return jnp.split(m, 6, axis=-1)


def modulate(x, g, shift, scale):
    return rmsnorm(x, g) * (1.0 + scale) + shift


def sq_relu_mlp(h, w_up, w_down):
    u = jax.nn.relu(h @ w_up)
    return (u * u) @ w_down


def dilated_prompt(q, k, v, win, dil):
    b, s, h, dh = q.shape
    span = win // dil
    L = s // dil
    nb = -(-L // span)
    Lp = nb * span

    def to_sub(x):
        x = x.reshape(b, L, dil, h, dh).transpose(0, 2, 3, 1, 4)
        x = jnp.pad(x, ((0, 0), (0, 0), (0, 0), (0, Lp - L), (0, 0)))
        return x.reshape(b, dil, h, nb, span, dh)

    def with_prev(x):
        prev = jnp.pad(x, ((0, 0), (0, 0), (0, 0), (1, 0), (0, 0), (0, 0)))[:, :, :, :-1]
        return jnp.concatenate([prev, x], axis=4)

    qs = to_sub(q)
    kk = with_prev(to_sub(k))
    vv = with_prev(to_sub(v))
    sc = jnp.einsum('brhnqd,brhnkd->brhnqk', qs, kk).astype(jnp.float32) * (dh ** -0.5)
    qi = jnp.arange(span)[:, None]
    kj = jnp.arange(2 * span)[None, :]
    dist = span + qi - kj
    blk = jnp.arange(nb)[:, None, None]
    valid = (dist >= 0) & (dist <= span) & ((blk > 0) | (kj >= span))
    sc = jnp.where(valid, sc, -jnp.inf)
    m = jnp.max(sc, axis=-1, keepdims=True)
    p = jnp.exp(sc - m)
    den = jnp.sum(p, axis=-1, keepdims=True)
    o = jnp.einsum('brhnqk,brhnkd->brhnqd', (p / den).astype(v.dtype), vv)
    lse = (m + jnp.log(den))[..., 0]

    def from_sub(x):
        tail = x.shape[5:]
        x = x.reshape((b, dil, h, Lp) + tail)[:, :, :, :L]
        x = jnp.moveaxis(x, 3, 1)
        return x.reshape((b, s, h) + tail)

    return from_sub(o), from_sub(lse)


def dilated_sample(q, k_new, v_new, kv_buf, win, dil):
    lw, t = kv_buf.shape[1], q.shape[1]
    span = win // dil
    k_all = jnp.concatenate([kv_buf[:, :, 0].astype(k_new.dtype), k_new], axis=1)
    v_all = jnp.concatenate([kv_buf[:, :, 1].astype(v_new.dtype), v_new], axis=1)
    idx = lw + jnp.arange(t)[:, None] - dil * jnp.arange(span + 1)[None, :]
    valid = idx >= 0
    idx = jnp.maximum(idx, 0)
    kg = k_all[:, idx]
    vg = v_all[:, idx]
    sc = jnp.einsum('bthd,btjhd->bhtj', q, kg).astype(jnp.float32) * (q.shape[-1] ** -0.5)
    sc = jnp.where(valid[None, None], sc, -jnp.inf)
    m = jnp.max(sc, axis=-1, keepdims=True)
    p = jnp.exp(sc - m)
    den = jnp.sum(p, axis=-1, keepdims=True)
    o = jnp.einsum('bhtj,btjhd->bthd', (p / den).astype(v_new.dtype), vg)
    lse = (m + jnp.log(den))[..., 0].transpose(0, 2, 1)
    keep = min(win, lw + t)
    new_buf = jnp.stack([k_all[:, -keep:], v_all[:, -keep:]], axis=2)
    return o, lse, new_buf


def mix_dilations(outs, lses):
    w = jax.nn.softmax(jnp.stack(lses, axis=0), axis=0)
    o = jnp.einsum('gbth,gbthd->bthd', w, jnp.stack(outs, axis=0).astype(jnp.float32))
    return o.astype(outs[0].dtype)


def gla_chunk(state, chunk):
    q, k, v, log_a = chunk
    q, k, v = (u.astype(jnp.float32) for u in (q, k, v))
    c = q.shape[1]
    cum = jnp.cumsum(log_a, axis=1)
    o_inter = jnp.einsum('bchk,bhkv->bchv', q * jnp.exp(cum), state)
    causal = (jnp.arange(c)[:, None] >= jnp.arange(c)[None, :])[None, :, :, None, None]
    rel = jnp.where(causal, cum[:, :, None] - cum[:, None, :], -jnp.inf)
    att = jnp.einsum('bihk,bjhk,bijhk->bhij', q, k, jnp.exp(rel))
    o_intra = jnp.einsum('bhij,bjhv->bihv', att, v)
    last = cum[:, -1]
    k_dec = k * jnp.exp(last[:, None] - cum)
    state = jnp.exp(last)[..., None] * state + jnp.einsum('bjhk,bjhv->bhkv', k_dec, v)
    return state, o_inter + o_intra


def gla_prompt(q, k, v, log_a):
    bsz, t = q.shape[:2]
    n = t // B_CHUNK

    def chunks(u):
        return jnp.moveaxis(u.reshape((bsz, n, B_CHUNK) + u.shape[2:]), 1, 0)

    s0 = jnp.zeros((bsz, B_HEADS, B_DK, B_DV), jnp.float32)
    s_fin, o = lax.scan(gla_chunk, s0, (chunks(q), chunks(k), chunks(v), chunks(log_a)))
    o = jnp.moveaxis(o, 0, 1).reshape(bsz, t, B_HEADS, B_DV)
    return o.astype(v.dtype), s_fin.astype(v.dtype)


def ab_project(h, w_in, w_a2, b_a2):
    bsz, t, _ = h.shape
    sizes = [A_W, A_W, A_W, B_HEADS * B_DK, B_HEADS * B_DK, B_OUT, B_OUT, B_GATE_RANK]
    aq, ak, av, bq, bk, bv, br, ba = jnp.split(h @ w_in, _split_points(sizes), axis=-1)
    a_shape = (bsz, t, len(A_BRANCHES), A_HEADS, HEAD_DIM)
    aq, ak, av = aq.reshape(a_shape), ak.reshape(a_shape), av.reshape(a_shape)
    bq = bq.reshape(bsz, t, B_HEADS, B_DK) * (B_DK ** -0.5)
    bk = bk.reshape(bsz, t, B_HEADS, B_DK)
    bv = bv.reshape(bsz, t, B_HEADS, B_DV)
    log_a = jax.nn.log_sigmoid((ba @ w_a2 + b_a2).astype(jnp.float32)) / B_GATE_TEMP
    log_a = log_a.reshape(bsz, t, B_HEADS, B_DK)
    return aq, ak, av, bq, bk, bv, br, log_a


def ab_output(o_a, o_b, br, g_bo, w_out):
    bsz, t = o_a.shape[:2]
    o_b = rmsnorm(o_b, g_bo.reshape(B_HEADS, B_DV)) * jax.nn.silu(br).reshape(bsz, t, B_HEADS, B_DV)
    mix = jnp.concatenate([o_a.reshape(bsz, t, A_OUT), o_b.reshape(bsz, t, B_OUT)], axis=-1)
    return mix @ w_out


def mixer_ab_prompt(h, w_in, w_a2, b_a2, g_bo, w_out):
    aq, ak, av, bq, bk, bv, br, log_a = ab_project(h, w_in, w_a2, b_a2)
    outs, lses, bufs = [], [], []
    for g, (win, dil) in enumerate(A_BRANCHES):
        o, lse = dilated_prompt(aq[:, :, g], ak[:, :, g], av[:, :, g], win, dil)
        outs.append(o)
        lses.append(lse)
        keep = min(win, h.shape[1])
        bufs.append(jnp.stack([ak[:, -keep:, g], av[:, -keep:, g]], axis=2))
    o_a = mix_dilations(outs, lses)
    o_b, s_b = gla_prompt(bq, bk, bv, log_a)
    return ab_output(o_a, o_b, br, g_bo, w_out), bufs, s_b


def mixer_ab_sample(h, a_bufs, s_b, w_in, w_a2, b_a2, g_bo, w_out):
    aq, ak, av, bq, bk, bv, br, log_a = ab_project(h, w_in, w_a2, b_a2)
    outs, lses, bufs = [], [], []
    for g, (win, dil) in enumerate(A_BRANCHES):
        o, lse, nb = dilated_sample(aq[:, :, g], ak[:, :, g], av[:, :, g], a_bufs[g], win, dil)
        outs.append(o)
        lses.append(lse)
        bufs.append(nb)
    o_a = mix_dilations(outs, lses)
    s_new, o_b = gla_chunk(s_b.astype(jnp.float32), (bq, bk, bv, log_a))
    o_b = o_b.astype(bv.dtype)
    return ab_output(o_a, o_b, br, g_bo, w_out), bufs, s_new.astype(bv.dtype)


def fox_project(h, w_in, b_f):
    bsz, t, _ = h.shape
    q, k, v, fl = jnp.split(h @ w_in, _split_points([C_W, C_W, C_W, C_HEADS]), axis=-1)
    shp = (bsz, t, C_HEADS, HEAD_DIM)
    logf = jax.nn.log_sigmoid((fl + b_f).astype(jnp.float32))
    return q.reshape(shp) * (HEAD_DIM ** -0.5), k.reshape(shp), v.reshape(shp), logf


def fox_prompt(q, k, v, logf):
    b, s, h, dh = q.shape
    cum_t = jnp.cumsum(logf, axis=1).transpose(0, 2, 1)
    kpos = jnp.arange(s)

    def block(i):
        start = i * C_QBLOCK
        qb = lax.dynamic_slice_in_dim(q, start, C_QBLOCK, axis=1)
        fq = lax.dynamic_slice_in_dim(cum_t, start, C_QBLOCK, axis=2)
        sc = jnp.einsum('bqhd,bkhd->bhqk', qb, k).astype(jnp.float32)
        sc = sc + fq[..., None] - cum_t[:, :, None, :]
        qpos = start + jnp.arange(C_QBLOCK)
        sc = jnp.where(qpos[:, None] >= kpos[None, :], sc, -jnp.inf)
        pr = jax.nn.softmax(sc, axis=-1)
        return jnp.einsum('bhqk,bkhd->bqhd', pr.astype(v.dtype), v)

    o = lax.map(block, jnp.arange(s // C_QBLOCK))
    return o.transpose(1, 0, 2, 3, 4).reshape(b, s, h, dh)


def fox_sample(q, k_new, v_new, logf_new, cache_k, cache_v, cache_logf, page_table, layer):
    db, t, h, dh = q.shape
    n_pages = page_table.shape[1]
    past = n_pages * PAGE_SIZE
    logf_past = cache_logf[layer, page_table].reshape(db, past, h).astype(jnp.float32)
    suffix = lax.cumsum(logf_past, axis=1, reverse=True) - logf_past
    g_new = jnp.cumsum(logf_new, axis=1)
    g_h = g_new.transpose(0, 2, 1)[..., None]
    ppb = math.gcd(n_pages, C_PAGES_PER_BLOCK)
    nblk = n_pages // ppb
    blen = ppb * PAGE_SIZE
    pt_blocks = page_table.reshape(db, nblk, ppb).transpose(1, 0, 2)
    suf_blocks = suffix.reshape(db, nblk, blen, h).transpose(1, 0, 2, 3)

    def step(carry, xs):
        m, l, acc = carry
        pt, suf = xs
        kb = cache_k[layer, pt].reshape(db, blen, h, dh)
        vb = cache_v[layer, pt].reshape(db, blen, h, dh)
        sc = jnp.einsum('bthd,blhd->bhtl', q, kb.astype(q.dtype)).astype(jnp.float32)
        sc = sc + g_h + suf.transpose(0, 2, 1)[:, :, None, :]
        m_new = jnp.maximum(m, jnp.max(sc, axis=-1))
        corr = jnp.exp(m - m_new)
        p = jnp.exp(sc - m_new[..., None])
        l = l * corr + jnp.sum(p, axis=-1)
        acc = acc * corr[..., None] + jnp.einsum('bhtl,blhd->bhtd', p, vb.astype(jnp.float32))
        return (m_new, l, acc), None

    init = (jnp.full((db, h, t), -jnp.inf, jnp.float32),
            jnp.zeros((db, h, t), jnp.float32),
            jnp.zeros((db, h, t, dh), jnp.float32))
    (m, l, acc), _ = lax.scan(step, init, (pt_blocks, suf_blocks))
    sc = jnp.einsum('bthd,bshd->bhts', q, k_new).astype(jnp.float32)
    sc = sc + g_h - g_new.transpose(0, 2, 1)[:, :, None, :]
    causal = jnp.arange(t)[:, None] >= jnp.arange(t)[None, :]
    sc = jnp.where(causal, sc, -jnp.inf)
    m_fin = jnp.maximum(m, jnp.max(sc, axis=-1))
    corr = jnp.exp(m - m_fin)
    p = jnp.exp(sc - m_fin[..., None])
    l = l * corr + jnp.sum(p, axis=-1)
    acc = acc * corr[..., None] + jnp.einsum('bhts,bshd->bhtd', p, v_new.astype(jnp.float32))
    return (acc / l[..., None]).transpose(0, 2, 1, 3).astype(v_new.dtype)


def mixer_fox_prompt(h, w_in, b_f, w_out):
    q, k, v, logf = fox_project(h, w_in, b_f)
    o = fox_prompt(q, k, v, logf)
    bsz, t = h.shape[:2]
    return o.reshape(bsz, t, C_W) @ w_out, k, v, logf


def mixer_fox_sample(h, cache_k, cache_v, cache_logf, page_table, layer, w_in, b_f, w_out):
    q, k, v, logf = fox_project(h, w_in, b_f)
    o = fox_sample(q, k, v, logf, cache_k, cache_v, cache_logf, page_table, layer)
    bsz, t = h.shape[:2]
    return o.reshape(bsz, t, C_W) @ w_out, k, v, logf


def setup_inputs(seed: int = 0) -> dict:
    key = jax.random.key(seed)
    keys = list(jax.random.split(key, 40))

    def nrm(shape, scale=1.0):
        return scale * jax.random.normal(keys.pop(), shape, jnp.float32)

    n_pages = PAST_LEN // PAGE_SIZE
    n_used = DEC_BATCH * n_pages
    n_pool = n_used + max(1, n_used // 4)
    page_table = jax.random.permutation(keys.pop(), n_pool)[:n_used].reshape(DEC_BATCH, n_pages).astype(jnp.int32)
    a_bufs = [nrm((N_EVEN, DEC_BATCH, min(win, PAST_LEN), 2, A_HEADS, HEAD_DIM)) for win, _ in A_BRANCHES]
    return {
        'x_prompt': nrm((BATCH, SEQ, D_MODEL)),
        'x_sample': nrm((DEC_BATCH, DEC_SEQ, D_MODEL)),
        'cache_a0_kv': a_bufs[0],
        'cache_a1_kv': a_bufs[1],
        'cache_a2_kv': a_bufs[2],
        'state_gla': nrm((N_EVEN, DEC_BATCH, B_HEADS, B_DK, B_DV), 0.3),
        'cache_c_k': nrm((N_ODD, n_pool, PAGE_SIZE, C_HEADS, HEAD_DIM)),
        'cache_c_v': nrm((N_ODD, n_pool, PAGE_SIZE, C_HEADS, HEAD_DIM)),
        'cache_c_logf': jax.nn.log_sigmoid(C_FORGET_BIAS + nrm((N_ODD, n_pool, PAGE_SIZE, C_HEADS))),
        'page_table': page_table,
        'c_prompt': nrm((BATCH, D_MODEL)),
        'c_sample': nrm((DEC_BATCH, D_MODEL)),
        'w_ada': nrm((DEPTH, D_MODEL, 6 * D_MODEL), 0.5 * D_MODEL ** -0.5),
        'b_ada': nrm((DEPTH, 6 * D_MODEL), 0.02),
        'g_mix': 1.0 + nrm((DEPTH, D_MODEL), 0.05),
        'g_mlp': 1.0 + nrm((DEPTH, D_MODEL), 0.05),
        'w_in_ab': nrm((N_EVEN, D_MODEL, IN_AB_WIDTH), D_MODEL ** -0.5),
        'w_gla_a2': nrm((N_EVEN, B_GATE_RANK, B_HEADS * B_DK), B_GATE_RANK ** -0.5),
        'b_gla_a2': nrm((N_EVEN, B_HEADS * B_DK), 0.1),
        'g_gla_out': 1.0 + nrm((N_EVEN, B_HEADS * B_DV), 0.05),
        'w_out_ab': nrm((N_EVEN, MIX_AB_WIDTH, D_MODEL), MIX_AB_WIDTH ** -0.5),
        'w_in_fox': nrm((N_ODD, D_MODEL, IN_FOX_WIDTH), D_MODEL ** -0.5),
        'b_fox_f': C_FORGET_BIAS + nrm((N_ODD, C_HEADS), 0.5),
        'w_out_fox': nrm((N_ODD, C_W, D_MODEL), C_W ** -0.5),
        'w_up': nrm((DEPTH, D_MODEL, D_FF), D_MODEL ** -0.5),
        'w_down': nrm((DEPTH, D_FF, D_MODEL), D_FF ** -0.5),
        'g_final': 1.0 + nrm((D_MODEL,), 0.05),
    }


def reference(x_prompt, x_sample, cache_a0_kv, cache_a1_kv, cache_a2_kv, state_gla,
              cache_c_k, cache_c_v, cache_c_logf, page_table, c_prompt, c_sample,
              w_ada, b_ada, g_mix, g_mlp, w_in_ab, w_gla_a2, b_gla_a2, g_gla_out, w_out_ab,
              w_in_fox, b_fox_f, w_out_fox, w_up, w_down, g_final):
    xp, xs = x_prompt, x_sample
    a_p, a_s = [[], [], []], [[], [], []]
    gla_p, gla_s = [], []
    ck_p, cv_p, cf_p, ck_s, cv_s, cf_s = [], [], [], [], [], []
    for layer in range(DEPTH):
        tp = adaln_terms(c_prompt, w_ada[layer], b_ada[layer])
        ts = adaln_terms(c_sample, w_ada[layer], b_ada[layer])
        hp = modulate(xp, g_mix[layer], tp[0], tp[1])
        hs = modulate(xs, g_mix[layer], ts[0], ts[1])
        if layer % 2 == 0:
            e = layer // 2
            yp, bufs_p, sp = mixer_ab_prompt(hp, w_in_ab[e], w_gla_a2[e], b_gla_a2[e], g_gla_out[e], w_out_ab[e])
            ys, bufs_s, ss = mixer_ab_sample(hs, (cache_a0_kv[e], cache_a1_kv[e], cache_a2_kv[e]), state_gla[e],
                                             w_in_ab[e], w_gla_a2[e], b_gla_a2[e], g_gla_out[e], w_out_ab[e])
            for g in range(len(A_BRANCHES)):
                a_p[g].append(bufs_p[g])
                a_s[g].append(bufs_s[g])
            gla_p.append(sp)
            gla_s.append(ss)
        else:
            o = layer // 2
            yp, kp, vp, fp = mixer_fox_prompt(hp, w_in_fox[o], b_fox_f[o], w_out_fox[o])
            ys, ks_, vs_, fs_ = mixer_fox_sample(hs, cache_c_k, cache_c_v, cache_c_logf, page_table, o,
                                                 w_in_fox[o], b_fox_f[o], w_out_fox[o])
            ck_p.append(kp)
            cv_p.append(vp)
            cf_p.append(fp)
            ck_s.append(ks_)
            cv_s.append(vs_)
            cf_s.append(fs_)
        xp = xp + tp[2] * yp
        xs = xs + ts[2] * ys
        hp = modulate(xp, g_mlp[layer], tp[3], tp[4])
        hs = modulate(xs, g_mlp[layer], ts[3], ts[4])
        xp = xp + tp[5] * sq_relu_mlp(hp, w_up[layer], w_down[layer])
        xs = xs + ts[5] * sq_relu_mlp(hs, w_up[layer], w_down[layer])
    y_prompt = rmsnorm(xp, g_final)
    y_sample = rmsnorm(xs, g_final)
    return (y_prompt, y_sample,
            jnp.stack(a_p[0], axis=0), jnp.stack(a_p[1], axis=0), jnp.stack(a_p[2], axis=0),
            jnp.stack(gla_p, axis=0),
            jnp.stack(ck_p, axis=0), jnp.stack(cv_p, axis=0), jnp.stack(cf_p, axis=0),
            jnp.stack(a_s[0], axis=0), jnp.stack(a_s[1], axis=0), jnp.stack(a_s[2], axis=0),
            jnp.stack(gla_s, axis=0),
            jnp.stack(ck_s, axis=0), jnp.stack(cv_s, axis=0), jnp.stack(cf_s, axis=0))
```

```python
import functools

import jax
import jax.numpy as jnp
from jax import lax
from jax.experimental import pallas as pl
from jax.experimental.pallas import tpu as pltpu

F32 = jnp.float32
BF16 = jnp.bfloat16
HIGHEST = lax.Precision.HIGHEST

HEAD_DIM = 64
RMS_EPS = 1e-6
A_BRANCHES = ((128, 1), (512, 4), (2048, 16))
A_HEADS = 8
A_BW = A_HEADS * HEAD_DIM
B_HEADS = 4
B_DK = 64
B_DV = 128
B_QK = B_HEADS * B_DK
B_OUT = B_HEADS * B_DV
B_GATE_RANK = 16
B_GATE_TEMP = 16.0
C_HEADS = 16
QK_SCALE = HEAD_DIM ** -0.5
NEG = -1e30

V7X_VMEM_BYTES = 64 * 1024 * 1024
NT_DIMS = (((1,), (1,)), ((), ()))
TN_DIMS = (((0,), (0,)), ((), ()))


def _params(sem, vmem_mb):
    assert vmem_mb * 1024 * 1024 < V7X_VMEM_BYTES
    return pltpu.CompilerParams(dimension_semantics=sem, vmem_limit_bytes=vmem_mb * 1024 * 1024)


def _iota(shape, dim):
    return lax.broadcasted_iota(jnp.int32, shape, dim)


def _div(x, n):
    assert n & (n - 1) == 0
    return x >> (n.bit_length() - 1)


def _mod(x, n):
    assert n & (n - 1) == 0
    return x & (n - 1)


def _resident(shape):
    nd = len(shape)
    return pl.BlockSpec(shape, lambda *_: (0,) * nd, pipeline_mode=pl.Buffered(1))


def _log_sigmoid(z):
    return jnp.minimum(z, 0.0) - jnp.log1p(jnp.exp(-jnp.abs(z)))


def _modulated(x, g, shift, scale):
    var = jnp.mean(x * x, axis=-1, keepdims=True)
    h = x * lax.rsqrt(var + RMS_EPS) * g
    return h * (1.0 + scale) + shift


def _mod_spec(mod, tm):
    d = mod.shape[-1]
    if mod.shape[1] == 1:
        return pl.BlockSpec((1, 1, d), lambda g, i: (g, 0, 0))
    return pl.BlockSpec((1, tm, d), lambda g, i: (g, i, 0))


def _ada_kernel(c_ref, w_ref, b_ref, o_ref):
    o_ref[0] = jnp.dot(c_ref[...].astype(BF16), w_ref[0].astype(BF16),
                       preferred_element_type=F32) + b_ref[0]


def _adaln(c_all, w_ada, b_ada, tn=1024):
    nl, d, n = w_ada.shape
    nb = c_all.shape[0]
    return pl.pallas_call(
        _ada_kernel,
        grid=(nl, n // tn),
        in_specs=[pl.BlockSpec((nb, d), lambda l, j: (0, 0)),
                  pl.BlockSpec((1, d, tn), lambda l, j: (l, 0, j)),
                  pl.BlockSpec((1, 1, tn), lambda l, j: (l, 0, j))],
        out_specs=pl.BlockSpec((1, nb, tn), lambda l, j: (l, 0, j)),
        out_shape=jax.ShapeDtypeStruct((nl, nb, n), F32),
        compiler_params=_params(("arbitrary", "arbitrary"), 32),
        name="adaln",
    )(c_all, w_ada, b_ada.reshape(nl, 1, n))


def _ab_in_kernel(x_ref, g_ref, sh_ref, sc_ref, wq_ref, wkv0_ref, wkv1_ref, wkv2_ref, wb_ref, wba_ref,
                  wa2_ref, ba2_ref, q_ref, kv0_ref, kv1_ref, kv2_ref, b_ref, la_ref):
    hb = _modulated(x_ref[0], g_ref[...], sh_ref[0], sc_ref[0]).astype(BF16)
    q_ref[0] = (jnp.dot(hb, wq_ref[...], preferred_element_type=F32) * QK_SCALE).astype(q_ref.dtype)
    kv0_ref[0] = jnp.dot(hb, wkv0_ref[...], preferred_element_type=F32)
    kv1_ref[0] = jnp.dot(hb, wkv1_ref[...], preferred_element_type=F32)
    kv2_ref[0] = jnp.dot(hb, wkv2_ref[...], preferred_element_type=F32)
    b_ref[0] = jnp.dot(hb, wb_ref[...], preferred_element_type=F32)
    ba = jnp.dot(hb, wba_ref[...], preferred_element_type=F32)
    z = jnp.dot(ba.astype(BF16), wa2_ref[...], preferred_element_type=F32) + ba2_ref[...]
    la_ref[0] = _log_sigmoid(z) * (1.0 / B_GATE_TEMP)


def _ab_in(x, g, shift, scale, w, tm):
    ng, t, d = x.shape
    widths = [3 * A_BW, 2 * A_BW, 2 * A_BW, 2 * A_BW, 2 * B_QK + 2 * B_OUT, B_QK]
    dtypes = [BF16, F32, F32, F32, F32, F32]
    tok = lambda n: pl.BlockSpec((1, tm, n), lambda gi, i: (gi, i, 0))
    weights = [w["q"], w["kv0"], w["kv1"], w["kv2"], w["b"], w["ba"], w["a2"], w["ba2"]]
    return pl.pallas_call(
        _ab_in_kernel,
        grid=(ng, t // tm),
        in_specs=[tok(d), pl.BlockSpec((1, d), lambda gi, i: (0, 0)), _mod_spec(shift, tm), _mod_spec(scale, tm)]
                 + [_resident(a.shape) for a in weights],
        out_specs=[tok(n) for n in widths],
        out_shape=[jax.ShapeDtypeStruct((ng, t, n), dt) for n, dt in zip(widths, dtypes)],
        compiler_params=_params(("arbitrary", "arbitrary"), 56),
        name="ab_in_proj",
    )(x, g, shift, scale, *weights)


def _dil_prompt_kernel(q_ref, kc_ref, vc_ref, kp_ref, vp_ref, o_ref, lse_ref, *, span):
    n = pl.program_id(2)
    q = q_ref[0]
    kc = kc_ref[0].astype(BF16)
    vc = vc_ref[0].astype(BF16)
    kp = kp_ref[0].astype(BF16)
    vp = vp_ref[0].astype(BF16)
    qi = _iota((span, span), 0)
    kj = _iota((span, span), 1)
    cur_ok = kj <= qi
    prev_ok = kj >= qi
    prev_off = jnp.where(n > 0, 0.0, NEG)
    outs, lses = [], []
    for h in range(A_HEADS):
        sl = slice(h * HEAD_DIM, (h + 1) * HEAD_DIM)
        sc = lax.dot_general(q[:, sl], kc[:, sl], NT_DIMS, preferred_element_type=F32)
        sp = lax.dot_general(q[:, sl], kp[:, sl], NT_DIMS, preferred_element_type=F32)
        sc = jnp.where(cur_ok, sc, NEG)
        sp = jnp.where(prev_ok, sp + prev_off, NEG)
        m = jnp.maximum(jnp.max(sc, axis=-1, keepdims=True), jnp.max(sp, axis=-1, keepdims=True))
        pc = jnp.exp(sc - m)
        pp = jnp.exp(sp - m)
        den = jnp.sum(pc, axis=-1, keepdims=True) + jnp.sum(pp, axis=-1, keepdims=True)
        inv = 1.0 / den
        o = jnp.dot((pc * inv).astype(BF16), vc[:, sl], preferred_element_type=F32)
        o = o + jnp.dot((pp * inv).astype(BF16), vp[:, sl], preferred_element_type=F32)
        outs.append(o)
        lses.append(jnp.broadcast_to(m + jnp.log(den), (span, HEAD_DIM)))
    o_ref[0] = jnp.concatenate(outs, axis=1)
    lse_ref[0] = jnp.concatenate(lses, axis=1)


def _dil_prompt(q_all, kv, branch):
    win, dil = A_BRANCHES[branch]
    nbatch, s, _ = q_all.shape
    span = win // dil
    rows = s // dil
    assert s % dil == 0 and rows % span == 0
    nblk = rows // span
    q_v = q_all.reshape(nbatch, rows, dil * 3 * A_BW)
    kv_v = kv.reshape(nbatch, rows, dil * 2 * A_BW)
    blk = (1, span, A_BW)
    out_sds = jax.ShapeDtypeStruct((nbatch, rows, dil * A_BW), F32)
    o, lse = pl.pallas_call(
        functools.partial(_dil_prompt_kernel, span=span),
        grid=(nbatch, dil, nblk),
        in_specs=[pl.BlockSpec(blk, lambda b, r, n: (b, n, 3 * r + branch)),
                  pl.BlockSpec(blk, lambda b, r, n: (b, n, 2 * r)),
                  pl.BlockSpec(blk, lambda b, r, n: (b, n, 2 * r + 1)),
                  pl.BlockSpec(blk, lambda b, r, n: (b, jnp.maximum(n - 1, 0), 2 * r)),
                  pl.BlockSpec(blk, lambda b, r, n: (b, jnp.maximum(n - 1, 0), 2 * r + 1))],
        out_specs=[pl.BlockSpec(blk, lambda b, r, n: (b, n, r))] * 2,
        out_shape=[out_sds, out_sds],
        compiler_params=_params(("arbitrary",) * 3, 32),
        name=f"dilated_prompt_{branch}",
    )(q_v, kv_v, kv_v, kv_v, kv_v)
    return o.reshape(nbatch, s, A_BW), lse.reshape(nbatch, s, A_BW)


def _dil_sample_kernel(q_ref, kvn0_ref, kvn1_ref, kvn2_ref, buf0_ref, buf1_ref, buf2_ref, o_ref, *, t_new):
    rows = A_HEADS * t_new
    q = q_ref[0].astype(F32)
    own_head = _div(_iota((rows, A_BW), 0), t_new) == _div(_iota((rows, A_BW), 1), HEAD_DIM)
    outs, lses = [], []
    for g, (kvn_ref, buf_ref) in enumerate(((kvn0_ref, buf0_ref), (kvn1_ref, buf1_ref), (kvn2_ref, buf2_ref))):
        win, dil = A_BRANCHES[g]
        lw = buf_ref.shape[1]
        qg = jnp.concatenate([q[:, g * A_BW:(g + 1) * A_BW]] * A_HEADS, axis=0)
        qbd = jnp.where(own_head, qg, 0.0).astype(BF16)
        kb = buf_ref[0, :, :A_BW].astype(BF16)
        vb = buf_ref[0, :, A_BW:].astype(BF16)
        pad = jnp.zeros((16 - t_new, A_BW), F32)
        kn = jnp.concatenate([kvn_ref[0, :, :A_BW], pad], axis=0).astype(BF16)
        vn = jnp.concatenate([kvn_ref[0, :, A_BW:], pad], axis=0).astype(BF16)
        s1 = lax.dot_general(qbd, kb, NT_DIMS, preferred_element_type=F32)
        s2 = lax.dot_general(qbd, kn, NT_DIMS, preferred_element_type=F32)
        delta1 = lw + _mod(_iota((rows, lw), 0), t_new) - _iota((rows, lw), 1)
        ok1 = (_mod(delta1, dil) == 0) & (delta1 <= win)
        delta2 = _mod(_iota((rows, 16), 0), t_new) - _iota((rows, 16), 1)
        ok2 = (_mod(delta2, dil) == 0) & (delta2 >= 0)
        s1 = jnp.where(ok1, s1, NEG)
        s2 = jnp.where(ok2, s2, NEG)
        m = jnp.maximum(jnp.max(s1, axis=-1, keepdims=True), jnp.max(s2, axis=-1, keepdims=True))
        p1 = jnp.exp(s1 - m)
        p2 = jnp.exp(s2 - m)
        den = jnp.sum(p1, axis=-1, keepdims=True) + jnp.sum(p2, axis=-1, keepdims=True)
        inv = 1.0 / den
        o = jnp.dot((p1 * inv).astype(BF16), vb, preferred_element_type=F32)
        o = o + jnp.dot((p2 * inv).astype(BF16), vn, preferred_element_type=F32)
        outs.append(o)
        lses.append(m + jnp.log(den))
    mx = jnp.maximum(jnp.maximum(lses[0], lses[1]), lses[2])
    es = [jnp.exp(l - mx) for l in lses]
    inv = 1.0 / (es[0] + es[1] + es[2])
    mixed = (es[0] * outs[0] + es[1] * outs[1] + es[2] * outs[2]) * inv
    mixed = jnp.where(own_head, mixed, 0.0)
    acc = mixed[0:t_new]
    for h in range(1, A_HEADS):
        acc = acc + mixed[h * t_new:(h + 1) * t_new]
    o_ref[0] = acc.astype(o_ref.dtype)


def _dil_sample(q, kv_new, bufs):
    db, t_new, _ = q.shape
    assert t_new == 8
    for (win, _), buf in zip(A_BRANCHES, bufs):
        assert buf.shape[1] == win
    per_b = lambda shape: pl.BlockSpec((1,) + shape[1:], lambda b: (b, 0, 0))
    return pl.pallas_call(
        functools.partial(_dil_sample_kernel, t_new=t_new),
        grid=(db,),
        in_specs=[per_b(q.shape)] + [per_b(a.shape) for a in kv_new] + [per_b(a.shape) for a in bufs],
        out_specs=pl.BlockSpec((1, t_new, A_BW), lambda b: (b, 0, 0)),
        out_shape=jax.ShapeDtypeStruct((db, t_new, A_BW), BF16),
        compiler_params=_params(("arbitrary",), 48),
        name="dilated_sample",
    )(q, *kv_new, *bufs)


def _gla_kernel(qk_ref, v_ref, r_ref, la_ref, s0_ref, gbo_ref, o_ref, st_ref, *, chunk, sub):
    @pl.when(pl.program_id(1) == 0)
    def _():
        st_ref[0] = s0_ref[0]

    q = qk_ref[0, :, :B_QK] * (B_DK ** -0.5)
    k = qk_ref[0, :, B_QK:]
    v = v_ref[0]
    la = la_ref[0]
    rows_in = chunk
    if chunk < sub:
        grow = lambda a: jnp.concatenate([a, jnp.zeros((sub - chunk, a.shape[1]), F32)], axis=0)
        q, k, v, la = grow(q), grow(k), grow(v), grow(la)
        chunk = sub
    tri = (_iota((chunk, chunk), 0) >= _iota((chunk, chunk), 1)).astype(F32)
    cum = jnp.dot(tri, la, precision=HIGHEST, preferred_element_type=F32)
    last = cum[chunk - 1:chunk, :]
    st = st_ref[0]
    qe = (q * jnp.exp(cum)).astype(BF16)
    kd = (k * jnp.exp(last - cum)).astype(BF16)
    vb = v.astype(BF16)
    o_heads, st_heads = [], []
    for h in range(B_HEADS):
        ks = slice(h * B_DK, (h + 1) * B_DK)
        vs = slice(h * B_DV, (h + 1) * B_DV)
        st_h = st[:, ks]
        o_h = lax.dot_general(qe[:, ks], st_h.astype(BF16), NT_DIMS, preferred_element_type=F32)
        blocks = [jnp.zeros((sub, B_DV), F32)]
        for r0 in range(sub, chunk, sub):
            ref = cum[r0:r0 + 1, ks]
            qt = (q[r0:r0 + sub, ks] * jnp.exp(cum[r0:r0 + sub, ks] - ref)).astype(BF16)
            kt = (k[:r0, ks] * jnp.exp(ref - cum[:r0, ks])).astype(BF16)
            att = lax.dot_general(qt, kt, NT_DIMS, preferred_element_type=F32)
            blocks.append(jnp.dot(att.astype(BF16), vb[:r0, vs], preferred_element_type=F32))
        if len(blocks) > 1:
            o_h = o_h + jnp.concatenate(blocks, axis=0)
        o_heads.append(o_h)
        upd = lax.dot_general(vb[:, vs], kd[:, ks], TN_DIMS, preferred_element_type=F32)
        st_heads.append(st_h * jnp.exp(last[:, ks]) + upd)
    st_ref[0] = jnp.concatenate(st_heads, axis=1)
    o = jnp.concatenate(o_heads, axis=1)

    spread = (_div(_iota((B_QK, B_OUT), 0), B_DK) == _div(_iota((B_QK, B_OUT), 1), B_DV)).astype(BF16)
    row_in_block = _mod(_iota((chunk, B_QK), 0), sub)
    for off in range(sub):
        k_s = k if off == 0 else pltpu.roll(k, off, 0)
        c_s = cum if off == 0 else pltpu.roll(cum, off, 0)
        v_s = v if off == 0 else pltpu.roll(v, off, 0)
        term = q * k_s * jnp.exp(jnp.minimum(cum - c_s, 0.0))
        term = jnp.where(row_in_block >= off, term, 0.0)
        att = jnp.dot(term.astype(BF16), spread, preferred_element_type=F32)
        o = o + att * v_s

    o = o[:rows_in]
    r = r_ref[0]
    gate = r * (1.0 / (1.0 + jnp.exp(-r)))
    normed = []
    for h in range(B_HEADS):
        vs = slice(h * B_DV, (h + 1) * B_DV)
        o_h = o[:, vs]
        var = jnp.mean(o_h * o_h, axis=-1, keepdims=True)
        normed.append(o_h * lax.rsqrt(var + RMS_EPS))
    o_ref[0] = (jnp.concatenate(normed, axis=1) * gbo_ref[...] * gate).astype(o_ref.dtype)


def _gla(bproj, log_a, s0_t, g_bo, chunk, sub):
    nb, t, _ = bproj.shape
    assert t % chunk == 0 and (chunk % sub == 0 or (chunk < sub and t == chunk))
    half = 2 * B_QK
    assert half == B_OUT
    col = lambda j: pl.BlockSpec((1, chunk, half), lambda b, c: (b, c, j))
    st_spec = pl.BlockSpec((1, B_DV, B_QK), lambda b, c: (b, 0, 0))
    return pl.pallas_call(
        functools.partial(_gla_kernel, chunk=chunk, sub=sub),
        grid=(nb, t // chunk),
        in_specs=[col(0), col(1), col(2),
                  pl.BlockSpec((1, chunk, B_QK), lambda b, c: (b, c, 0)),
                  st_spec,
                  pl.BlockSpec((1, B_OUT), lambda b, c: (0, 0))],
        out_specs=[pl.BlockSpec((1, chunk, B_OUT), lambda b, c: (b, c, 0)), st_spec],
        out_shape=[jax.ShapeDtypeStruct((nb, t, B_OUT), BF16), jax.ShapeDtypeStruct((nb, B_DV, B_QK), F32)],
        compiler_params=_params(("arbitrary", "arbitrary"), 32),
        name=f"gla_chunk{chunk}",
    )(bproj, bproj, bproj, log_a, s0_t, g_bo)


def _proj_res_kernel(*refs, n_in, mix3):
    if mix3:
        o0, o1, o2, l0, l1, l2 = (r[0] for r in refs[:6])
        refs = refs[6:]
        mx = jnp.maximum(jnp.maximum(l0, l1), l2)
        e0, e1, e2 = jnp.exp(l0 - mx), jnp.exp(l1 - mx), jnp.exp(l2 - mx)
        first = ((e0 * o0 + e1 * o1 + e2 * o2) * (1.0 / (e0 + e1 + e2))).astype(BF16)
        ins = [first] + [r[0] for r in refs[:n_in - 1]]
        refs = refs[n_in - 1:]
    else:
        ins = [r[0] for r in refs[:n_in]]
        refs = refs[n_in:]
    w_refs, (x_ref, gate_ref, out_ref) = refs[:n_in], refs[n_in:]
    y = jnp.dot(ins[0], w_refs[0][...], preferred_element_type=F32)
    for a, w_ref in zip(ins[1:], w_refs[1:]):
        y = y + jnp.dot(a, w_ref[...], preferred_element_type=F32)
    out_ref[0] = x_ref[0] + gate_ref[0] * y


def _proj_res(ins, ws, x, gate, tm, mix=None):
    ng, t, d = x.shape
    tok = lambda a: pl.BlockSpec((1, tm, a.shape[-1]), lambda gi, i: (gi, i, 0))
    lead = list(mix) if mix is not None else []
    n_in = len(ws)
    return pl.pallas_call(
        functools.partial(_proj_res_kernel, n_in=n_in, mix3=mix is not None),
        grid=(ng, t // tm),
        in_specs=[tok(a) for a in lead] + [tok(a) for a in ins] + [_resident(w.shape) for w in ws]
                 + [tok(x), _mod_spec(gate, tm)],
        out_specs=tok(x),
        out_shape=jax.ShapeDtypeStruct(x.shape, F32),
        compiler_params=_params(("arbitrary", "arbitrary"), 48),
        name="proj_residual_mix" if mix is not None else "proj_residual",
    )(*lead, *ins, *ws, x, gate)


def _mlp_kernel(x_ref, g_ref, sh_ref, sc_ref, gate_ref, wu_ref, wd_ref, gf_ref, out_ref, *, fchunk, final_norm):
    x = x_ref[0]
    hb = _modulated(x, g_ref[...], sh_ref[0], sc_ref[0]).astype(BF16)
    acc = jnp.zeros(x.shape, F32)
    for f0 in range(0, wu_ref.shape[1], fchunk):
        u = jnp.maximum(jnp.dot(hb, wu_ref[:, f0:f0 + fchunk], preferred_element_type=F32), 0.0)
        acc = acc + jnp.dot((u * u).astype(BF16), wd_ref[f0:f0 + fchunk, :], preferred_element_type=F32)
    y = x + gate_ref[0] * acc
    if final_norm:
        var = jnp.mean(y * y, axis=-1, keepdims=True)
        y = y * lax.rsqrt(var + RMS_EPS) * gf_ref[...]
    out_ref[0] = y


def _mlp(x, g, shift, scale, gate, w_up, w_down, g_final, tm, final_norm):
    ng, t, d = x.shape
    tok = pl.BlockSpec((1, tm, d), lambda gi, i: (gi, i, 0))
    vec = pl.BlockSpec((1, d), lambda gi, i: (0, 0))
    return pl.pallas_call(
        functools.partial(_mlp_kernel, fchunk=1024, final_norm=final_norm),
        grid=(ng, t // tm),
        in_specs=[tok, vec, _mod_spec(shift, tm), _mod_spec(scale, tm), _mod_spec(gate, tm),
                  _resident(w_up.shape), _resident(w_down.shape), vec],
        out_specs=tok,
        out_shape=jax.ShapeDtypeStruct(x.shape, F32),
        compiler_params=_params(("arbitrary", "arbitrary"), 56),
        name="mlp_final" if final_norm else "mlp",
    )(x, g, shift, scale, gate, w_up, w_down, g_final)


def _fox_in_kernel(x_ref, g_ref, sh_ref, sc_ref, wq_ref, wk_ref, wv_ref, wf_ref, bf_ref,
                   q_ref, k_ref, v_ref, lf_ref, cum_ref, carry_ref, *, tm):
    @pl.when(pl.program_id(1) == 0)
    def _():
        carry_ref[...] = jnp.zeros_like(carry_ref)

    hb = _modulated(x_ref[0], g_ref[...], sh_ref[0], sc_ref[0]).astype(BF16)
    q_ref[0] = (jnp.dot(hb, wq_ref[...], preferred_element_type=F32) * QK_SCALE).astype(q_ref.dtype)
    k_ref[0] = jnp.dot(hb, wk_ref[...], preferred_element_type=F32)
    v_ref[0] = jnp.dot(hb, wv_ref[...], preferred_element_type=F32)
    lf = _log_sigmoid(jnp.dot(hb, wf_ref[...], preferred_element_type=F32) + bf_ref[...])
    lf_ref[0] = lf
    tri = (_iota((tm, tm), 0) >= _iota((tm, tm), 1)).astype(F32)
    cum = jnp.dot(tri, lf, precision=HIGHEST, preferred_element_type=F32) + carry_ref[...]
    cum_ref[0] = cum
    carry_ref[...] = cum[tm - 1:tm, :]


def _fox_in(x, g, shift, scale, w, tm):
    ng, t, d = x.shape
    cw = C_HEADS * HEAD_DIM
    widths = [cw, cw, cw, C_HEADS, C_HEADS]
    dtypes = [BF16, F32, F32, F32, F32]
    tok = lambda n: pl.BlockSpec((1, tm, n), lambda gi, i: (gi, i, 0))
    weights = [w["q"], w["k"], w["v"], w["f"], w["bf"]]
    return pl.pallas_call(
        functools.partial(_fox_in_kernel, tm=tm),
        grid=(ng, t // tm),
        in_specs=[tok(d), pl.BlockSpec((1, d), lambda gi, i: (0, 0)), _mod_spec(shift, tm), _mod_spec(scale, tm)]
                 + [_resident(a.shape) for a in weights],
        out_specs=[tok(n) for n in widths],
        out_shape=[jax.ShapeDtypeStruct((ng, t, n), dt) for n, dt in zip(widths, dtypes)],
        scratch_shapes=[pltpu.VMEM((1, C_HEADS), F32)],
        compiler_params=_params(("arbitrary", "arbitrary"), 48),
        name="fox_in_proj",
    )(x, g, shift, scale, *weights)


def _fox_prompt_kernel(q_ref, k_ref, v_ref, cq_ref, ck_ref, o_ref, m_ref, l_ref, acc_ref, *, tq):
    qi = pl.program_id(1)
    ki = pl.program_id(2)

    @pl.when(ki == 0)
    def _():
        m_ref[...] = jnp.full(m_ref.shape, NEG, F32)
        l_ref[...] = jnp.zeros_like(l_ref)
        acc_ref[...] = jnp.zeros_like(acc_ref)

    def update(diagonal):
        q = q_ref[0]
        kb = k_ref[0].astype(BF16)
        vb = v_ref[0].astype(BF16)
        cq = cq_ref[0]
        ck = ck_ref[0]
        causal = _iota((tq, tq), 0) >= _iota((tq, tq), 1)
        for h in range(C_HEADS):
            sl = slice(h * HEAD_DIM, (h + 1) * HEAD_DIM)
            s = lax.dot_general(q[:, sl], kb[:, sl], NT_DIMS, preferred_element_type=F32)
            s = s + cq[:, h:h + 1] - ck[h:h + 1, :]
            if diagonal:
                s = jnp.where(causal, s, NEG)
            m_prev = m_ref[h]
            m_new = jnp.maximum(m_prev, jnp.max(s, axis=-1, keepdims=True))
            corr = jnp.exp(m_prev - m_new)
            p = jnp.exp(s - m_new)
            l_new = l_ref[h] * corr + jnp.sum(p, axis=-1, keepdims=True)
            acc = acc_ref[:, sl] * corr + jnp.dot(p.astype(BF16), vb[:, sl], preferred_element_type=F32)
            m_ref[h] = m_new
            if diagonal:
                o_ref[0, :, sl] = (acc * (1.0 / l_new)).astype(o_ref.dtype)
            else:
                l_ref[h] = l_new
                acc_ref[:, sl] = acc

    @pl.when(ki < qi)
    def _():
        update(False)

    @pl.when(ki == qi)
    def _():
        update(True)


def _fox_prompt(q, k, v, cum, cum_t, tq):
    nb, s, w = q.shape
    nq = s // tq
    kv_spec = pl.BlockSpec((1, tq, w), lambda b, i, j: (b, jnp.minimum(i, j), 0))
    return pl.pallas_call(
        functools.partial(_fox_prompt_kernel, tq=tq),
        grid=(nb, nq, nq),
        in_specs=[pl.BlockSpec((1, tq, w), lambda b, i, j: (b, i, 0)), kv_spec, kv_spec,
                  pl.BlockSpec((1, tq, C_HEADS), lambda b, i, j: (b, i, 0)),
                  pl.BlockSpec((1, C_HEADS, tq), lambda b, i, j: (b, 0, jnp.minimum(i, j)))],
        out_specs=pl.BlockSpec((1, tq, w), lambda b, i, j: (b, i, 0)),
        out_shape=jax.ShapeDtypeStruct((nb, s, w), BF16),
        scratch_shapes=[pltpu.VMEM((C_HEADS, tq, 1), F32), pltpu.VMEM((C_HEADS, tq, 1), F32),
                        pltpu.VMEM((tq, w), F32)],
        compiler_params=_params(("arbitrary",) * 3, 48),
        name="fox_prompt",
    )(q, k, v, cum, cum_t)


def _fox_sample_kernel(pt_ref, q_ref, kn_ref, vn_ref, lfn_ref, ck_ref, cv_ref, clf_ref, o_ref,
                       qbd_ref, gq_ref, carry_ref, m_ref, l_ref, acc_ref, *, t_new, n_pages, page):
    del pt_ref
    p_idx = pl.program_id(1)
    rows = C_HEADS * t_new
    w = C_HEADS * HEAD_DIM
    npad = 16
    own_head = _div(_iota((rows, w), 0), t_new) == _div(_iota((rows, w), 1), HEAD_DIM)
    spread = (_iota((C_HEADS, rows), 0) == _div(_iota((C_HEADS, rows), 1), t_new)).astype(F32)
    spread_t = (_div(_iota((rows, C_HEADS), 0), t_new) == _iota((rows, C_HEADS), 1)).astype(F32)

    def new_token_bias():
        lfn = jnp.concatenate([lfn_ref[0], jnp.zeros((npad - t_new, C_HEADS), F32)], axis=0)
        tri = (_iota((npad, npad), 0) >= _iota((npad, npad), 1)).astype(F32)
        g_new = jnp.dot(tri, lfn, precision=HIGHEST, preferred_element_type=F32)
        gk = lax.dot_general(spread_t, g_new, NT_DIMS, precision=HIGHEST, preferred_element_type=F32)
        is_own_t = _mod(_iota((rows, npad), 0), t_new) == _iota((rows, npad), 1)
        gq = jnp.sum(jnp.where(is_own_t, gk, 0.0), axis=-1, keepdims=True)
        return gq, gk

    @pl.when(p_idx == 0)
    def _():
        qt = jnp.concatenate([q_ref[0].astype(F32)] * C_HEADS, axis=0)
        qbd_ref[...] = jnp.where(own_head, qt, 0.0).astype(BF16)
        gq, _ = new_token_bias()
        gq_ref[...] = jnp.broadcast_to(gq, gq_ref.shape)
        carry_ref[...] = jnp.zeros_like(carry_ref)
        m_ref[...] = jnp.full(m_ref.shape, NEG, F32)
        l_ref[...] = jnp.zeros_like(l_ref)
        acc_ref[...] = jnp.zeros_like(acc_ref)

    kb = ck_ref[0].astype(BF16)
    vb = cv_ref[0].astype(BF16)
    lf_rows = jnp.dot(clf_ref[0], spread, precision=HIGHEST, preferred_element_type=F32)
    later = jnp.where(_iota((page, 2 * page), 1) < page,
                      (_iota((page, 2 * page), 0) > _iota((page, 2 * page), 1)).astype(F32), 1.0)
    sums = lax.dot_general(lf_rows, later, TN_DIMS, precision=HIGHEST, preferred_element_type=F32)
    bias = sums[:, :page] + carry_ref[...] + gq_ref[...]
    carry_ref[...] = carry_ref[...] + sums[:, page:]
    s = lax.dot_general(qbd_ref[...], kb, NT_DIMS, preferred_element_type=F32) + bias
    m_prev = m_ref[...]
    m_new = jnp.maximum(m_prev, jnp.max(s, axis=-1, keepdims=True))
    corr = jnp.exp(m_prev - m_new)
    p = jnp.exp(s - m_new)
    l_ref[...] = l_ref[...] * corr + jnp.sum(p, axis=-1, keepdims=True)
    acc_ref[...] = acc_ref[...] * corr + jnp.dot(p.astype(BF16), vb, preferred_element_type=F32)
    m_ref[...] = m_new

    @pl.when(p_idx == n_pages - 1)
    def _():
        gq, gk = new_token_bias()
        pad = jnp.zeros((npad - t_new, w), F32)
        kn = jnp.concatenate([kn_ref[0], pad], axis=0).astype(BF16)
        vn = jnp.concatenate([vn_ref[0], pad], axis=0).astype(BF16)
        s2 = lax.dot_general(qbd_ref[...], kn, NT_DIMS, preferred_element_type=F32) + gq - gk
        ok = _mod(_iota((rows, npad), 0), t_new) >= _iota((rows, npad), 1)
        s2 = jnp.where(ok, s2, NEG)
        m_prev2 = m_ref[...]
        m_fin = jnp.maximum(m_prev2, jnp.max(s2, axis=-1, keepdims=True))
        corr2 = jnp.exp(m_prev2 - m_fin)
        p2 = jnp.exp(s2 - m_fin)
        l_fin = l_ref[...] * corr2 + jnp.sum(p2, axis=-1, keepdims=True)
        acc = acc_ref[...] * corr2 + jnp.dot(p2.astype(BF16), vn, preferred_element_type=F32)
        acc = jnp.where(own_head, acc * (1.0 / l_fin), 0.0)
        out = acc[0:t_new]
        for h in range(1, C_HEADS):
            out = out + acc[h * t_new:(h + 1) * t_new]
        o_ref[0] = out.astype(o_ref.dtype)


def _fox_sample(q, k_new, v_new, lf_new, cache_k, cache_v, cache_lf, page_table, layer_off):
    db, t_new, w = q.shape
    n_pages = page_table.shape[1]
    page = cache_k.shape[1]
    rows = C_HEADS * t_new
    assert t_new == 8 and page == rows
    per_b = lambda n: pl.BlockSpec((1, t_new, n), lambda b, p, pt: (b, 0, 0))
    paged = lambda n: pl.BlockSpec((1, page, n), lambda b, p, pt: (layer_off + pt[b, n_pages - 1 - p], 0, 0))
    return pl.pallas_call(
        functools.partial(_fox_sample_kernel, t_new=t_new, n_pages=n_pages, page=page),
        grid_spec=pltpu.PrefetchScalarGridSpec(
            num_scalar_prefetch=1,
            grid=(db, n_pages),
            in_specs=[per_b(w), per_b(w), per_b(w), per_b(C_HEADS), paged(w), paged(w), paged(C_HEADS)],
            out_specs=per_b(w),
            scratch_shapes=[pltpu.VMEM((rows, w), BF16), pltpu.VMEM((rows, page), F32),
                            pltpu.VMEM((rows, page), F32), pltpu.VMEM((rows, 1), F32),
                            pltpu.VMEM((rows, 1), F32), pltpu.VMEM((rows, w), F32)]),
        out_shape=jax.ShapeDtypeStruct((db, t_new, w), BF16),
        compiler_params=_params(("arbitrary", "arbitrary"), 32),
        name="fox_sample",
    )(page_table, q, k_new, v_new, lf_new, cache_k, cache_v, cache_lf)


def _split_cols(w, sizes):
    out, a = [], 0
    for s in sizes:
        out.append(w[:, a:a + s])
        a += s
    return out


def _ab_weights(w_in, w_a2, b_a2):
    aw = len(A_BRANCHES) * A_BW
    aq, ak, av, bq, bk, bv, br, ba = _split_cols(w_in, [aw, aw, aw, B_QK, B_QK, B_OUT, B_OUT, B_GATE_RANK])
    w = {"q": aq.astype(BF16), "b": jnp.concatenate([bq, bk, bv, br], axis=1).astype(BF16),
         "ba": ba.astype(BF16), "a2": w_a2.astype(BF16), "ba2": b_a2.reshape(1, -1)}
    for g in range(len(A_BRANCHES)):
        cols = slice(g * A_BW, (g + 1) * A_BW)
        w[f"kv{g}"] = jnp.concatenate([ak[:, cols], av[:, cols]], axis=1).astype(BF16)
    return w


def _fox_weights(w_in, b_f):
    cw = C_HEADS * HEAD_DIM
    q, k, v, f = _split_cols(w_in, [cw, cw, cw, C_HEADS])
    return {"q": q.astype(BF16), "k": k.astype(BF16), "v": v.astype(BF16), "f": f.astype(BF16),
            "bf": b_f.reshape(1, -1)}


def kernel(x_prompt, x_sample, cache_a0_kv, cache_a1_kv, cache_a2_kv, state_gla, cache_c_k, cache_c_v, cache_c_logf,
           page_table, c_prompt, c_sample, w_ada, b_ada, g_mix, g_mlp, w_in_ab, w_gla_a2, b_gla_a2, g_gla_out, w_out_ab,
           w_in_fox, b_fox_f, w_out_fox, w_up, w_down, g_final):
    nb, seq, d = x_prompt.shape
    db, t_new, _ = x_sample.shape
    depth = w_ada.shape[0]
    n_pool, page = cache_c_k.shape[1], cache_c_k.shape[2]
    tm_p, tm_s = 512, db * t_new

    ada = _adaln(jnp.concatenate([c_prompt, c_sample], axis=0), w_ada, b_ada)

    def terms(layer):
        tp = [ada[layer, :nb, i * d:(i + 1) * d][:, None, :] for i in range(6)]
        ts = [jnp.repeat(ada[layer, nb:, i * d:(i + 1) * d], t_new, axis=0)[None] for i in range(6)]
        return tp, ts

    xp = x_prompt
    xs = x_sample.reshape(1, db * t_new, d)
    a_caches = (cache_a0_kv, cache_a1_kv, cache_a2_kv)
    a_p, a_s = [[], [], []], [[], [], []]
    gla_p, gla_s = [], []
    ck_p, cv_p, cf_p, ck_s, cv_s, cf_s = [], [], [], [], [], []
    g_final2 = g_final.reshape(1, d)

    for layer in range(depth):
        tp, ts = terms(layer)
        g1 = g_mix[layer].reshape(1, d)
        g2 = g_mlp[layer].reshape(1, d)
        last = layer == depth - 1
        if layer % 2 == 0:
            e = layer // 2
            w = _ab_weights(w_in_ab[e], w_gla_a2[e], b_gla_a2[e])
            w_out_a = w_out_ab[e, :A_BW].astype(BF16)
            w_out_b = w_out_ab[e, A_BW:].astype(BF16)
            g_bo = g_gla_out[e].reshape(1, B_OUT)

            q, kv0, kv1, kv2, bproj, log_a = _ab_in(xp, g1, tp[0], tp[1], w, tm_p)
            kvs = (kv0, kv1, kv2)
            branch = [_dil_prompt(q, kvs[g], g) for g in range(len(A_BRANCHES))]
            o_b, st = _gla(bproj, log_a, jnp.zeros((nb, B_DV, B_QK), F32), g_bo, chunk=64, sub=16)
            mix = [o for o, _ in branch] + [l for _, l in branch]
            xp = _proj_res([o_b], [w_out_a, w_out_b], xp, tp[2], tm_p, mix=mix)
            for g, (win, _) in enumerate(A_BRANCHES):
                keep = min(win, seq)
                a_p[g].append(kvs[g][:, seq - keep:].reshape(nb, keep, 2, A_HEADS, HEAD_DIM))
            gla_p.append(st.reshape(nb, B_DV, B_HEADS, B_DK).transpose(0, 2, 3, 1))

            q, kv0, kv1, kv2, bproj, log_a = _ab_in(xs, g1, ts[0], ts[1], w, tm_s)
            kv_new = [a.reshape(db, t_new, 2 * A_BW) for a in (kv0, kv1, kv2)]
            bufs = [c[e].reshape(db, c.shape[2], 2 * A_BW) for c in a_caches]
            o_a = _dil_sample(q.reshape(db, t_new, 3 * A_BW), kv_new, bufs)
            s0_t = state_gla[e].transpose(0, 3, 1, 2).reshape(db, B_DV, B_QK)
            o_b, st = _gla(bproj.reshape(db, t_new, -1), log_a.reshape(db, t_new, B_QK), s0_t, g_bo,
                           chunk=t_new, sub=16)
            xs = _proj_res([o_a.reshape(1, db * t_new, A_BW), o_b.reshape(1, db * t_new, B_OUT)],
                           [w_out_a, w_out_b], xs, ts[2], tm_s)
            for g, (win, _) in enumerate(A_BRANCHES):
                keep = min(win, bufs[g].shape[1] + t_new)
                full = jnp.concatenate([bufs[g], kv_new[g]], axis=1)
                a_s[g].append(full[:, full.shape[1] - keep:].reshape(db, keep, 2, A_HEADS, HEAD_DIM))
            gla_s.append(st.reshape(db, B_DV, B_HEADS, B_DK).transpose(0, 2, 3, 1))
        else:
            o_idx = layer // 2
            w = _fox_weights(w_in_fox[o_idx], b_fox_f[o_idx])
            w_out = w_out_fox[o_idx].astype(BF16)
            cw = C_HEADS * HEAD_DIM

            q, k, v, lf, cum = _fox_in(xp, g1, tp[0], tp[1], w, tm_p)
            o = _fox_prompt(q, k, v, cum, cum.transpose(0, 2, 1), tq=512)
            xp = _proj_res([o], [w_out], xp, tp[2], tm_p)
            ck_p.append(k.reshape(nb, seq, C_HEADS, HEAD_DIM))
            cv_p.append(v.reshape(nb, seq, C_HEADS, HEAD_DIM))
            cf_p.append(lf)

            q, k, v, lf, _ = _fox_in(xs, g1, ts[0], ts[1], w, tm_s)
            o = _fox_sample(q.reshape(db, t_new, cw), k.reshape(db, t_new, cw), v.reshape(db, t_new, cw),
                            lf.reshape(db, t_new, C_HEADS),
                            cache_c_k.reshape(-1, page, cw), cache_c_v.reshape(-1, page, cw),
                            cache_c_logf.reshape(-1, page, C_HEADS), page_table, o_idx * n_pool)
            xs = _proj_res([o.reshape(1, db * t_new, cw)], [w_out], xs, ts[2], tm_s)
            ck_s.append(k.reshape(db, t_new, C_HEADS, HEAD_DIM))
            cv_s.append(v.reshape(db, t_new, C_HEADS, HEAD_DIM))
            cf_s.append(lf.reshape(db, t_new, C_HEADS))

        w_up_b = w_up[layer].astype(BF16)
        w_down_b = w_down[layer].astype(BF16)
        xp = _mlp(xp, g2, tp[3], tp[4], tp[5], w_up_b, w_down_b, g_final2, tm_p, last)
        xs = _mlp(xs, g2, ts[3], ts[4], ts[5], w_up_b, w_down_b, g_final2, tm_s, last)

    stack = lambda parts: jnp.stack(parts, axis=0)
    return (xp, xs.reshape(db, t_new, d),
            stack(a_p[0]), stack(a_p[1]), stack(a_p[2]), stack(gla_p),
            stack(ck_p), stack(cv_p), stack(cf_p),
            stack(a_s[0]), stack(a_s[1]), stack(a_s[2]), stack(gla_s),
            stack(ck_s), stack(cv_s), stack(cf_s))
```

```python
import functools

import jax
import jax.numpy as jnp
from jax import lax
from jax.experimental import pallas as pl
from jax.experimental.pallas import tpu as pltpu

F32 = jnp.float32
BF16 = jnp.bfloat16
HIGHEST = lax.Precision.HIGHEST

HEAD_DIM = 64
RMS_EPS = 1e-6
A_BRANCHES = ((128, 1), (512, 4), (2048, 16))
A_HEADS = 8
A_BW = A_HEADS * HEAD_DIM
B_HEADS = 4
B_DK = 64
B_DV = 128
B_QK = B_HEADS * B_DK
B_OUT = B_HEADS * B_DV
B_GATE_RANK = 16
B_GATE_TEMP = 16.0
C_HEADS = 16
QK_SCALE = HEAD_DIM ** -0.5
NEG = -1e30

V7X_VMEM_BYTES = 64 * 1024 * 1024
NT_DIMS = (((1,), (1,)), ((), ()))
TN_DIMS = (((0,), (0,)), ((), ()))


def _params(sem, vmem_mb):
    assert vmem_mb * 1024 * 1024 < V7X_VMEM_BYTES
    return pltpu.CompilerParams(dimension_semantics=sem, vmem_limit_bytes=vmem_mb * 1024 * 1024)


def _iota(shape, dim):
    return lax.broadcasted_iota(jnp.int32, shape, dim)


def _div(x, n):
    assert n & (n - 1) == 0
    return x >> (n.bit_length() - 1)


def _mod(x, n):
    assert n & (n - 1) == 0
    return x & (n - 1)


def _bdot_nn(a, b):
    return lax.dot_general(a, b, (((2,), (1,)), ((0,), (0,))), preferred_element_type=F32)


def _bdot_nt(a, b):
    return lax.dot_general(a, b, (((2,), (2,)), ((0,), (0,))), preferred_element_type=F32)


def _resident(shape):
    nd = len(shape)
    return pl.BlockSpec(shape, lambda *_: (0,) * nd, pipeline_mode=pl.Buffered(1))


def _log_sigmoid(z):
    return jnp.minimum(z, 0.0) - jnp.log1p(jnp.exp(-jnp.abs(z)))


def _modulated(x, g, shift, scale):
    var = jnp.mean(x * x, axis=-1, keepdims=True)
    h = x * lax.rsqrt(var + RMS_EPS) * g
    return h * (1.0 + scale) + shift


def _mod_spec(mod, tm):
    d = mod.shape[-1]
    if mod.shape[1] == 1:
        return pl.BlockSpec((1, 1, d), lambda g, i: (g, 0, 0))
    return pl.BlockSpec((1, tm, d), lambda g, i: (g, i, 0))


def _ada_kernel(c_ref, w_ref, b_ref, o_ref):
    o_ref[0] = jnp.dot(c_ref[...].astype(BF16), w_ref[0].astype(BF16),
                       preferred_element_type=F32) + b_ref[0]


def _adaln(c_all, w_ada, b_ada, tn=1024):
    nl, d, n = w_ada.shape
    nb = c_all.shape[0]
    return pl.pallas_call(
        _ada_kernel,
        grid=(nl, n // tn),
        in_specs=[pl.BlockSpec((nb, d), lambda l, j: (0, 0)),
                  pl.BlockSpec((1, d, tn), lambda l, j: (l, 0, j)),
                  pl.BlockSpec((1, 1, tn), lambda l, j: (l, 0, j))],
        out_specs=pl.BlockSpec((1, nb, tn), lambda l, j: (l, 0, j)),
        out_shape=jax.ShapeDtypeStruct((nl, nb, n), F32),
        compiler_params=_params(("arbitrary", "arbitrary"), 32),
        name="adaln",
    )(c_all, w_ada, b_ada.reshape(nl, 1, n))


def _ab_in_kernel(x_ref, g_ref, sh_ref, sc_ref, wq_ref, wkv0_ref, wkv1_ref, wkv2_ref, wb_ref, wba_ref,
                  wa2_ref, ba2_ref, q_ref, kv0_ref, kv1_ref, kv2_ref, b_ref, la_ref):
    hb = _modulated(x_ref[0], g_ref[...], sh_ref[0], sc_ref[0]).astype(BF16)
    q_ref[0] = (jnp.dot(hb, wq_ref[...], preferred_element_type=F32) * QK_SCALE).astype(q_ref.dtype)
    kv0_ref[0] = jnp.dot(hb, wkv0_ref[...], preferred_element_type=F32)
    kv1_ref[0] = jnp.dot(hb, wkv1_ref[...], preferred_element_type=F32)
    kv2_ref[0] = jnp.dot(hb, wkv2_ref[...], preferred_element_type=F32)
    b_ref[0] = jnp.dot(hb, wb_ref[...], preferred_element_type=F32)
    ba = jnp.dot(hb, wba_ref[...], preferred_element_type=F32)
    z = jnp.dot(ba.astype(BF16), wa2_ref[...], preferred_element_type=F32) + ba2_ref[...]
    la_ref[0] = _log_sigmoid(z) * (1.0 / B_GATE_TEMP)


def _ab_in(x, g, shift, scale, w, tm):
    ng, t, d = x.shape
    widths = [3 * A_BW, 2 * A_BW, 2 * A_BW, 2 * A_BW, 2 * B_QK + 2 * B_OUT, B_QK]
    dtypes = [BF16, F32, F32, F32, F32, F32]
    tok = lambda n: pl.BlockSpec((1, tm, n), lambda gi, i: (gi, i, 0))
    weights = [w["q"], w["kv0"], w["kv1"], w["kv2"], w["b"], w["ba"], w["a2"], w["ba2"]]
    return pl.pallas_call(
        _ab_in_kernel,
        grid=(ng, t // tm),
        in_specs=[tok(d), pl.BlockSpec((1, d), lambda gi, i: (0, 0)), _mod_spec(shift, tm), _mod_spec(scale, tm)]
                 + [_resident(a.shape) for a in weights],
        out_specs=[tok(n) for n in widths],
        out_shape=[jax.ShapeDtypeStruct((ng, t, n), dt) for n, dt in zip(widths, dtypes)],
        compiler_params=_params(("arbitrary", "arbitrary"), 56),
        name="ab_in_proj",
    )(x, g, shift, scale, *weights)


def _dil_prompt_kernel(q_ref, kc_ref, vc_ref, kp_ref, vp_ref, o_ref, lse_ref, *, span):
    n = pl.program_id(2)
    q = q_ref[0]
    kc = kc_ref[0].astype(BF16)
    vc = vc_ref[0].astype(BF16)
    kp = kp_ref[0].astype(BF16)
    vp = vp_ref[0].astype(BF16)
    qi = _iota((span, span), 0)
    kj = _iota((span, span), 1)
    cur_ok = kj <= qi
    prev_ok = kj >= qi
    prev_off = jnp.where(n > 0, 0.0, NEG)
    outs, lses = [], []
    for h in range(A_HEADS):
        sl = slice(h * HEAD_DIM, (h + 1) * HEAD_DIM)
        sc = lax.dot_general(q[:, sl], kc[:, sl], NT_DIMS, preferred_element_type=F32)
        sp = lax.dot_general(q[:, sl], kp[:, sl], NT_DIMS, preferred_element_type=F32)
        sc = jnp.where(cur_ok, sc, NEG)
        sp = jnp.where(prev_ok, sp + prev_off, NEG)
        m = jnp.maximum(jnp.max(sc, axis=-1, keepdims=True), jnp.max(sp, axis=-1, keepdims=True))
        pc = jnp.exp(sc - m)
        pp = jnp.exp(sp - m)
        den = jnp.sum(pc, axis=-1, keepdims=True) + jnp.sum(pp, axis=-1, keepdims=True)
        inv = 1.0 / den
        o = jnp.dot((pc * inv).astype(BF16), vc[:, sl], preferred_element_type=F32)
        o = o + jnp.dot((pp * inv).astype(BF16), vp[:, sl], preferred_element_type=F32)
        outs.append(o)
        lses.append(jnp.broadcast_to(m + jnp.log(den), (span, HEAD_DIM)))
    o_ref[0] = jnp.concatenate(outs, axis=1)
    lse_ref[0] = jnp.concatenate(lses, axis=1)


def _dil_prompt(q_all, kv, branch):
    win, dil = A_BRANCHES[branch]
    nbatch, s, _ = q_all.shape
    span = win // dil
    rows = s // dil
    assert s % dil == 0 and rows % span == 0
    nblk = rows // span
    q_v = q_all.reshape(nbatch, rows, dil * 3 * A_BW)
    kv_v = kv.reshape(nbatch, rows, dil * 2 * A_BW)
    blk = (1, span, A_BW)
    out_sds = jax.ShapeDtypeStruct((nbatch, rows, dil * A_BW), F32)
    o, lse = pl.pallas_call(
        functools.partial(_dil_prompt_kernel, span=span),
        grid=(nbatch, dil, nblk),
        in_specs=[pl.BlockSpec(blk, lambda b, r, n: (b, n, 3 * r + branch)),
                  pl.BlockSpec(blk, lambda b, r, n: (b, n, 2 * r)),
                  pl.BlockSpec(blk, lambda b, r, n: (b, n, 2 * r + 1)),
                  pl.BlockSpec(blk, lambda b, r, n: (b, jnp.maximum(n - 1, 0), 2 * r)),
                  pl.BlockSpec(blk, lambda b, r, n: (b, jnp.maximum(n - 1, 0), 2 * r + 1))],
        out_specs=[pl.BlockSpec(blk, lambda b, r, n: (b, n, r))] * 2,
        out_shape=[out_sds, out_sds],
        compiler_params=_params(("arbitrary",) * 3, 32),
        name=f"dilated_prompt_{branch}",
    )(q_v, kv_v, kv_v, kv_v, kv_v)
    return o.reshape(nbatch, s, A_BW), lse.reshape(nbatch, s, A_BW)


def _dil_sample_kernel(q_ref, new0_ref, new1_ref, new2_ref, buf0_ref, buf1_ref, buf2_ref, o_ref):
    outs, lses = [], []
    for g, (new_ref, buf_ref) in enumerate(((new0_ref, buf0_ref), (new1_ref, buf1_ref), (new2_ref, buf2_ref))):
        win, dil = A_BRANCHES[g]
        lw = buf_ref.shape[-1]
        npad = new_ref.shape[-1]
        q = q_ref[0, g]
        t_new = q.shape[1]
        s1 = _bdot_nn(q, buf_ref[0, 0].astype(BF16))
        s2 = _bdot_nn(q, new_ref[0, 0].astype(BF16))
        delta1 = lw + _iota(s1.shape, 1) - _iota(s1.shape, 2)
        delta2 = _iota(s2.shape, 1) - _iota(s2.shape, 2)
        s1 = jnp.where((_mod(delta1, dil) == 0) & (delta1 <= win), s1, NEG)
        s2 = jnp.where((_mod(delta2, dil) == 0) & (delta2 >= 0), s2, NEG)
        m = jnp.maximum(jnp.max(s1, axis=-1, keepdims=True), jnp.max(s2, axis=-1, keepdims=True))
        p1 = jnp.exp(s1 - m)
        p2 = jnp.exp(s2 - m)
        den = jnp.sum(p1, axis=-1, keepdims=True) + jnp.sum(p2, axis=-1, keepdims=True)
        inv = 1.0 / den
        o = _bdot_nt((p1 * inv).astype(BF16), buf_ref[0, 1].astype(BF16))
        o = o + _bdot_nt((p2 * inv).astype(BF16), new_ref[0, 1].astype(BF16))
        outs.append(o)
        lses.append(m + jnp.log(den))
    mx = jnp.maximum(jnp.maximum(lses[0], lses[1]), lses[2])
    es = [jnp.exp(l - mx) for l in lses]
    inv = 1.0 / (es[0] + es[1] + es[2])
    o_ref[0] = (es[0] * outs[0] + es[1] * outs[1] + es[2] * outs[2]) * inv


def _dil_sample(q, new_t, bufs_t):
    db, _, nh, t_new, dh = q.shape
    for (win, _), buf in zip(A_BRANCHES, bufs_t):
        assert buf.shape[-1] == win
    per_b = lambda a: pl.BlockSpec((1,) + a.shape[1:], lambda b: (b,) + (0,) * (a.ndim - 1))
    return pl.pallas_call(
        _dil_sample_kernel,
        grid=(db,),
        in_specs=[per_b(q)] + [per_b(a) for a in new_t] + [per_b(a) for a in bufs_t],
        out_specs=pl.BlockSpec((1, nh, t_new, dh), lambda b: (b, 0, 0, 0)),
        out_shape=jax.ShapeDtypeStruct((db, nh, t_new, dh), F32),
        compiler_params=_params(("arbitrary",), 48),
        name="dilated_sample",
    )(q, *new_t, *bufs_t)


def _gla_kernel(qk_ref, v_ref, r_ref, la_ref, s0_ref, gbo_ref, o_ref, st_ref, *, chunk, sub):
    @pl.when(pl.program_id(1) == 0)
    def _():
        st_ref[0] = s0_ref[0]

    q = qk_ref[0, :, :B_QK] * (B_DK ** -0.5)
    k = qk_ref[0, :, B_QK:]
    v = v_ref[0]
    la = la_ref[0]
    rows_in = chunk
    if chunk < sub:
        grow = lambda a: jnp.concatenate([a, jnp.zeros((sub - chunk, a.shape[1]), F32)], axis=0)
        q, k, v, la = grow(q), grow(k), grow(v), grow(la)
        chunk = sub
    tri = (_iota((chunk, chunk), 0) >= _iota((chunk, chunk), 1)).astype(F32)
    cum = jnp.dot(tri, la, precision=HIGHEST, preferred_element_type=F32)
    last = cum[chunk - 1:chunk, :]
    st = st_ref[0]
    qe = (q * jnp.exp(cum)).astype(BF16)
    kd = (k * jnp.exp(last - cum)).astype(BF16)
    vb = v.astype(BF16)
    o_heads, st_heads = [], []
    for h in range(B_HEADS):
        ks = slice(h * B_DK, (h + 1) * B_DK)
        vs = slice(h * B_DV, (h + 1) * B_DV)
        st_h = st[:, ks]
        o_h = lax.dot_general(qe[:, ks], st_h.astype(BF16), NT_DIMS, preferred_element_type=F32)
        blocks = [jnp.zeros((sub, B_DV), F32)]
        for r0 in range(sub, chunk, sub):
            ref = cum[r0:r0 + 1, ks]
            qt = (q[r0:r0 + sub, ks] * jnp.exp(cum[r0:r0 + sub, ks] - ref)).astype(BF16)
            kt = (k[:r0, ks] * jnp.exp(ref - cum[:r0, ks])).astype(BF16)
            att = lax.dot_general(qt, kt, NT_DIMS, preferred_element_type=F32)
            blocks.append(jnp.dot(att.astype(BF16), vb[:r0, vs], preferred_element_type=F32))
        if len(blocks) > 1:
            o_h = o_h + jnp.concatenate(blocks, axis=0)
        o_heads.append(o_h)
        upd = lax.dot_general(vb[:, vs], kd[:, ks], TN_DIMS, preferred_element_type=F32)
        st_heads.append(st_h * jnp.exp(last[:, ks]) + upd)
    st_ref[0] = jnp.concatenate(st_heads, axis=1)
    o = jnp.concatenate(o_heads, axis=1)

    spread = (_div(_iota((B_QK, B_OUT), 0), B_DK) == _div(_iota((B_QK, B_OUT), 1), B_DV)).astype(BF16)
    row_in_block = _mod(_iota((chunk, B_QK), 0), sub)
    for off in range(sub):
        k_s = k if off == 0 else pltpu.roll(k, off, 0)
        c_s = cum if off == 0 else pltpu.roll(cum, off, 0)
        v_s = v if off == 0 else pltpu.roll(v, off, 0)
        term = q * k_s * jnp.exp(jnp.minimum(cum - c_s, 0.0))
        term = jnp.where(row_in_block >= off, term, 0.0)
        att = jnp.dot(term.astype(BF16), spread, preferred_element_type=F32)
        o = o + att * v_s

    o = o[:rows_in]
    r = r_ref[0]
    gate = r * (1.0 / (1.0 + jnp.exp(-r)))
    normed = []
    for h in range(B_HEADS):
        vs = slice(h * B_DV, (h + 1) * B_DV)
        o_h = o[:, vs]
        var = jnp.mean(o_h * o_h, axis=-1, keepdims=True)
        normed.append(o_h * lax.rsqrt(var + RMS_EPS))
    o_ref[0] = (jnp.concatenate(normed, axis=1) * gbo_ref[...] * gate).astype(o_ref.dtype)


def _gla(bproj, log_a, s0_t, g_bo, chunk, sub):
    nb, t, _ = bproj.shape
    assert t % chunk == 0 and (chunk % sub == 0 or (chunk < sub and t == chunk))
    half = 2 * B_QK
    assert half == B_OUT
    col = lambda j: pl.BlockSpec((1, chunk, half), lambda b, c: (b, c, j))
    st_spec = pl.BlockSpec((1, B_DV, B_QK), lambda b, c: (b, 0, 0))
    return pl.pallas_call(
        functools.partial(_gla_kernel, chunk=chunk, sub=sub),
        grid=(nb, t // chunk),
        in_specs=[col(0), col(1), col(2),
                  pl.BlockSpec((1, chunk, B_QK), lambda b, c: (b, c, 0)),
                  st_spec,
                  pl.BlockSpec((1, B_OUT), lambda b, c: (0, 0))],
        out_specs=[pl.BlockSpec((1, chunk, B_OUT), lambda b, c: (b, c, 0)), st_spec],
        out_shape=[jax.ShapeDtypeStruct((nb, t, B_OUT), BF16), jax.ShapeDtypeStruct((nb, B_DV, B_QK), F32)],
        compiler_params=_params(("arbitrary", "arbitrary"), 32),
        name=f"gla_chunk{chunk}",
    )(bproj, bproj, bproj, log_a, s0_t, g_bo)


def _proj_res_kernel(*refs, n_in, mix3):
    if mix3:
        o0, o1, o2, l0, l1, l2 = (r[0] for r in refs[:6])
        refs = refs[6:]
        mx = jnp.maximum(jnp.maximum(l0, l1), l2)
        e0, e1, e2 = jnp.exp(l0 - mx), jnp.exp(l1 - mx), jnp.exp(l2 - mx)
        first = ((e0 * o0 + e1 * o1 + e2 * o2) * (1.0 / (e0 + e1 + e2))).astype(BF16)
        ins = [first] + [r[0] for r in refs[:n_in - 1]]
        refs = refs[n_in - 1:]
    else:
        ins = [r[0] for r in refs[:n_in]]
        refs = refs[n_in:]
    w_refs, (x_ref, gate_ref, out_ref) = refs[:n_in], refs[n_in:]
    y = jnp.dot(ins[0], w_refs[0][...], preferred_element_type=F32)
    for a, w_ref in zip(ins[1:], w_refs[1:]):
        y = y + jnp.dot(a, w_ref[...], preferred_element_type=F32)
    out_ref[0] = x_ref[0] + gate_ref[0] * y


def _proj_res(ins, ws, x, gate, tm, mix=None):
    ng, t, d = x.shape
    tok = lambda a: pl.BlockSpec((1, tm, a.shape[-1]), lambda gi, i: (gi, i, 0))
    lead = list(mix) if mix is not None else []
    n_in = len(ws)
    return pl.pallas_call(
        functools.partial(_proj_res_kernel, n_in=n_in, mix3=mix is not None),
        grid=(ng, t // tm),
        in_specs=[tok(a) for a in lead] + [tok(a) for a in ins] + [_resident(w.shape) for w in ws]
                 + [tok(x), _mod_spec(gate, tm)],
        out_specs=tok(x),
        out_shape=jax.ShapeDtypeStruct(x.shape, F32),
        compiler_params=_params(("arbitrary", "arbitrary"), 48),
        name="proj_residual_mix" if mix is not None else "proj_residual",
    )(*lead, *ins, *ws, x, gate)


def _mlp_kernel(x_ref, g_ref, sh_ref, sc_ref, gate_ref, wu_ref, wd_ref, gf_ref, out_ref, *, fchunk, final_norm):
    x = x_ref[0]
    hb = _modulated(x, g_ref[...], sh_ref[0], sc_ref[0]).astype(BF16)
    acc = jnp.zeros(x.shape, F32)
    for f0 in range(0, wu_ref.shape[1], fchunk):
        u = jnp.maximum(jnp.dot(hb, wu_ref[:, f0:f0 + fchunk], preferred_element_type=F32), 0.0)
        acc = acc + jnp.dot((u * u).astype(BF16), wd_ref[f0:f0 + fchunk, :], preferred_element_type=F32)
    y = x + gate_ref[0] * acc
    if final_norm:
        var = jnp.mean(y * y, axis=-1, keepdims=True)
        y = y * lax.rsqrt(var + RMS_EPS) * gf_ref[...]
    out_ref[0] = y


def _mlp(x, g, shift, scale, gate, w_up, w_down, g_final, tm, final_norm):
    ng, t, d = x.shape
    tok = pl.BlockSpec((1, tm, d), lambda gi, i: (gi, i, 0))
    vec = pl.BlockSpec((1, d), lambda gi, i: (0, 0))
    return pl.pallas_call(
        functools.partial(_mlp_kernel, fchunk=1024, final_norm=final_norm),
        grid=(ng, t // tm),
        in_specs=[tok, vec, _mod_spec(shift, tm), _mod_spec(scale, tm), _mod_spec(gate, tm),
                  _resident(w_up.shape), _resident(w_down.shape), vec],
        out_specs=tok,
        out_shape=jax.ShapeDtypeStruct(x.shape, F32),
        compiler_params=_params(("arbitrary", "arbitrary"), 56),
        name="mlp_final" if final_norm else "mlp",
    )(x, g, shift, scale, gate, w_up, w_down, g_final)


def _fox_in_kernel(x_ref, g_ref, sh_ref, sc_ref, wq_ref, wkt_ref, wvt_ref, wf_ref, wft_ref, bf_ref, bft_ref,
                   q_ref, kt_ref, vt_ref, ktb_ref, vtb_ref, lft_ref, cumq_ref, cumt_ref, carry_row_ref, carry_col_ref,
                   *, tm):
    @pl.when(pl.program_id(1) == 0)
    def _():
        carry_row_ref[...] = jnp.zeros_like(carry_row_ref)
        carry_col_ref[...] = jnp.zeros_like(carry_col_ref)

    hb = _modulated(x_ref[0], g_ref[...], sh_ref[0], sc_ref[0]).astype(BF16)
    q_ref[0] = (jnp.dot(hb, wq_ref[...], preferred_element_type=F32) * QK_SCALE).astype(q_ref.dtype)
    kt = lax.dot_general(wkt_ref[...], hb, NT_DIMS, preferred_element_type=F32)
    kt_ref[0] = kt
    ktb_ref[0] = kt.astype(BF16)
    vt = lax.dot_general(wvt_ref[...], hb, NT_DIMS, preferred_element_type=F32)
    vt_ref[0] = vt
    vtb_ref[0] = vt.astype(BF16)
    lf = _log_sigmoid(jnp.dot(hb, wf_ref[...], preferred_element_type=F32) + bf_ref[...])
    lft = _log_sigmoid(lax.dot_general(wft_ref[...], hb, NT_DIMS, preferred_element_type=F32) + bft_ref[...])
    lft_ref[0] = lft
    upto = (_iota((tm, tm), 0) <= _iota((tm, tm), 1)).astype(F32)
    since = (_iota((tm, tm), 0) >= _iota((tm, tm), 1)).astype(F32)
    cumq = jnp.dot(since, lf, precision=HIGHEST, preferred_element_type=F32) + carry_row_ref[...]
    cumq_ref[0] = cumq
    carry_row_ref[...] = cumq[tm - 1:tm, :]
    cumt = jnp.dot(lft, upto, precision=HIGHEST, preferred_element_type=F32) + carry_col_ref[:, :1]
    cumt_ref[0] = cumt
    carry_col_ref[...] = jnp.broadcast_to(cumt[:, tm - 1:tm], carry_col_ref.shape)


def _fox_in(x, g, shift, scale, w, tm):
    ng, t, d = x.shape
    cw = C_HEADS * HEAD_DIM
    tok = lambda n: pl.BlockSpec((1, tm, n), lambda gi, i: (gi, i, 0))
    tok_t = lambda n: pl.BlockSpec((1, n, tm), lambda gi, i: (gi, 0, i))
    weights = [w["q"], w["kt"], w["vt"], w["f"], w["ft"], w["bf"], w["bft"]]
    sds = jax.ShapeDtypeStruct
    return pl.pallas_call(
        functools.partial(_fox_in_kernel, tm=tm),
        grid=(ng, t // tm),
        in_specs=[tok(d), pl.BlockSpec((1, d), lambda gi, i: (0, 0)), _mod_spec(shift, tm), _mod_spec(scale, tm)]
                 + [_resident(a.shape) for a in weights],
        out_specs=[tok(cw), tok_t(cw), tok_t(cw), tok_t(cw), tok_t(cw), tok_t(C_HEADS), tok(C_HEADS), tok_t(C_HEADS)],
        out_shape=[sds((ng, t, cw), BF16), sds((ng, cw, t), F32), sds((ng, cw, t), F32),
                   sds((ng, cw, t), BF16), sds((ng, cw, t), BF16),
                   sds((ng, C_HEADS, t), F32), sds((ng, t, C_HEADS), F32), sds((ng, C_HEADS, t), F32)],
        scratch_shapes=[pltpu.VMEM((1, C_HEADS), F32), pltpu.VMEM((C_HEADS, 128), F32)],
        compiler_params=_params(("arbitrary", "arbitrary"), 48),
        name="fox_in_proj",
    )(x, g, shift, scale, *weights)


FOX_TQ = 512


def _fox_prompt_kernel(q_ref, kt_ref, vt_ref, cq_ref, ck_ref, o_ref, m_ref, l_ref, acc_ref):
    qi = pl.program_id(1)
    ki = pl.program_id(2)
    tq = FOX_TQ

    @pl.when(ki == 0)
    def _():
        m_ref[...] = jnp.full(m_ref.shape, NEG, F32)
        l_ref[...] = jnp.zeros_like(l_ref)
        acc_ref[...] = jnp.zeros_like(acc_ref)

    def update(diagonal):
        q = q_ref[0]
        cq = cq_ref[0]
        ck = ck_ref[0]
        causal = _iota((tq, tq), 0) >= _iota((tq, tq), 1)
        for h in range(C_HEADS):
            sl = slice(h * HEAD_DIM, (h + 1) * HEAD_DIM)
            s = jnp.dot(q[:, sl], kt_ref[0, sl, :], preferred_element_type=F32)
            s = s + cq[:, h:h + 1] - ck[h:h + 1, :]
            if diagonal:
                s = jnp.where(causal, s, NEG)
            m_prev = m_ref[h]
            m_new = jnp.maximum(m_prev, jnp.max(s, axis=-1, keepdims=True))
            corr = jnp.exp(m_prev - m_new)
            p = jnp.exp(s - m_new)
            l_new = l_ref[h] * corr + jnp.sum(p, axis=-1, keepdims=True)
            pv = lax.dot_general(p.astype(BF16), vt_ref[0, sl, :], NT_DIMS, preferred_element_type=F32)
            acc = acc_ref[:, sl] * corr + pv
            m_ref[h] = m_new
            if diagonal:
                o_ref[0, :, sl] = (acc * (1.0 / l_new)).astype(o_ref.dtype)
            else:
                l_ref[h] = l_new
                acc_ref[:, sl] = acc

    @pl.when(ki < qi)
    def _():
        update(False)

    @pl.when(ki == qi)
    def _():
        update(True)


def _fox_prompt(q, kt, vt, cum, cum_t):
    nb, s, w = q.shape
    tq = FOX_TQ
    nq = s // tq
    kv_spec = pl.BlockSpec((1, w, tq), lambda b, i, j: (b, 0, jnp.minimum(i, j)))
    return pl.pallas_call(
        _fox_prompt_kernel,
        grid=(nb, nq, nq),
        in_specs=[pl.BlockSpec((1, tq, w), lambda b, i, j: (b, i, 0)), kv_spec, kv_spec,
                  pl.BlockSpec((1, tq, C_HEADS), lambda b, i, j: (b, i, 0)),
                  pl.BlockSpec((1, C_HEADS, tq), lambda b, i, j: (b, 0, jnp.minimum(i, j)))],
        out_specs=pl.BlockSpec((1, tq, w), lambda b, i, j: (b, i, 0)),
        out_shape=jax.ShapeDtypeStruct((nb, s, w), BF16),
        scratch_shapes=[pltpu.VMEM((C_HEADS, tq, 1), F32), pltpu.VMEM((C_HEADS, tq, 1), F32),
                        pltpu.VMEM((tq, w), F32)],
        compiler_params=_params(("arbitrary",) * 3, 48),
        name="fox_prompt",
    )(q, kt, vt, cum, cum_t)


def _rows_as_batch(x):
    return jnp.concatenate([x[h:h + 1][None] for h in range(x.shape[0])], axis=0)


def _fox_sample_kernel(pt_ref, q_ref, kn_ref, vn_ref, lfn_ref, *refs, n_pp, n_steps):
    del pt_ref
    k_refs, v_refs, lf_refs = refs[:n_pp], refs[n_pp:2 * n_pp], refs[2 * n_pp:3 * n_pp]
    o_ref, gq_ref, carry_ref, m_ref, l_ref, acc_ref = refs[3 * n_pp:]
    step = pl.program_id(1)
    q = q_ref[0]
    nh = q.shape[0]
    page = lf_refs[0].shape[-1]
    npad = lfn_ref.shape[-1]

    def new_token_cumsum():
        upto = (_iota((npad, npad), 0) <= _iota((npad, npad), 1)).astype(F32)
        return _rows_as_batch(jnp.dot(lfn_ref[0], upto, precision=HIGHEST, preferred_element_type=F32))

    def absorb(scores, values):
        m_prev = m_ref[...]
        m_new = m_prev
        for s in scores:
            m_new = jnp.maximum(m_new, jnp.max(s, axis=-1, keepdims=True))
        corr = jnp.exp(m_prev - m_new)
        l_new = l_ref[...] * corr
        acc = acc_ref[...] * corr
        for s, v_t in zip(scores, values):
            p = jnp.exp(s - m_new)
            l_new = l_new + jnp.sum(p, axis=-1, keepdims=True)
            acc = acc + _bdot_nt(p.astype(BF16), v_t)
        l_ref[...] = l_new
        acc_ref[...] = acc
        m_ref[...] = m_new

    @pl.when(step == 0)
    def _():
        g = jnp.broadcast_to(new_token_cumsum(), (nh, q.shape[1], npad))
        own_t = _iota(g.shape, 1) == _iota(g.shape, 2)
        gq_ref[...] = jnp.sum(jnp.where(own_t, g, 0.0), axis=-1, keepdims=True)
        carry_ref[...] = jnp.zeros_like(carry_ref)
        m_ref[...] = jnp.full(m_ref.shape, NEG, F32)
        l_ref[...] = jnp.zeros_like(l_ref)
        acc_ref[...] = jnp.zeros_like(acc_ref)

    lf_all = jnp.concatenate([r[0] for r in lf_refs], axis=0)
    later = jnp.where(_iota((page, 2 * page), 1) < page,
                      (_iota((page, 2 * page), 0) > _iota((page, 2 * page), 1)).astype(F32), 1.0)
    sums = jnp.dot(lf_all, later, precision=HIGHEST, preferred_element_type=F32)
    carry = carry_ref[...]
    scores = []
    for j in reversed(range(n_pp)):
        rows = slice(j * nh, (j + 1) * nh)
        bias = sums[rows, :page] + carry
        carry = carry + sums[rows, page:]
        scores.append(_bdot_nn(q, k_refs[j][0].astype(BF16)) + _rows_as_batch(bias) + gq_ref[...])
    carry_ref[...] = carry
    absorb(scores, [v_refs[j][0].astype(BF16) for j in reversed(range(n_pp))])

    @pl.when(step == n_steps - 1)
    def _():
        s = _bdot_nn(q, kn_ref[0].astype(BF16)) + gq_ref[...] - new_token_cumsum()
        causal = _iota(s.shape, 1) >= _iota(s.shape, 2)
        absorb([jnp.where(causal, s, NEG)], [vn_ref[0].astype(BF16)])
        o_ref[0] = acc_ref[...] * (1.0 / l_ref[...])


FOX_PAGES_PER_STEP = 8


def _fox_sample(q, k_new_t, v_new_t, lf_new_t, cache_k_t, cache_v_t, cache_lf_t, page_table, layer_off):
    db, nh, t_new, dh = q.shape
    n_pages = page_table.shape[1]
    page = cache_k_t.shape[-1]
    n_pp = FOX_PAGES_PER_STEP
    assert n_pages % n_pp == 0
    n_steps = n_pages // n_pp
    per_b = lambda a: pl.BlockSpec((1,) + a.shape[1:], lambda b, i, pt: (b,) + (0,) * (a.ndim - 1))

    def paged(a, j):
        return pl.BlockSpec((1,) + a.shape[1:],
                            lambda b, i, pt: (layer_off + pt[b, n_pages - n_pp * (i + 1) + j],) + (0,) * (a.ndim - 1))

    caches = (cache_k_t, cache_v_t, cache_lf_t)
    return pl.pallas_call(
        functools.partial(_fox_sample_kernel, n_pp=n_pp, n_steps=n_steps),
        grid_spec=pltpu.PrefetchScalarGridSpec(
            num_scalar_prefetch=1,
            grid=(db, n_steps),
            in_specs=[per_b(q), per_b(k_new_t), per_b(v_new_t), per_b(lf_new_t)]
                     + [paged(a, j) for a in caches for j in range(n_pp)],
            out_specs=pl.BlockSpec((1, nh, t_new, dh), lambda b, i, pt: (b, 0, 0, 0)),
            scratch_shapes=[pltpu.VMEM((nh, t_new, 1), F32), pltpu.VMEM((nh, page), F32),
                            pltpu.VMEM((nh, t_new, 1), F32), pltpu.VMEM((nh, t_new, 1), F32),
                            pltpu.VMEM((nh, t_new, dh), F32)]),
        out_shape=jax.ShapeDtypeStruct((db, nh, t_new, dh), F32),
        compiler_params=_params(("arbitrary", "arbitrary"), 40),
        name="fox_sample",
    )(page_table, q, k_new_t, v_new_t, lf_new_t, *[a for a in caches for _ in range(n_pp)])


def _split_cols(w, sizes):
    out, a = [], 0
    for s in sizes:
        out.append(w[:, a:a + s])
        a += s
    return out


def _ab_weights(w_in, w_a2, b_a2):
    aw = len(A_BRANCHES) * A_BW
    aq, ak, av, bq, bk, bv, br, ba = _split_cols(w_in, [aw, aw, aw, B_QK, B_QK, B_OUT, B_OUT, B_GATE_RANK])
    w = {"q": aq.astype(BF16), "b": jnp.concatenate([bq, bk, bv, br], axis=1).astype(BF16),
         "ba": ba.astype(BF16), "a2": w_a2.astype(BF16), "ba2": b_a2.reshape(1, -1)}
    for g in range(len(A_BRANCHES)):
        cols = slice(g * A_BW, (g + 1) * A_BW)
        w[f"kv{g}"] = jnp.concatenate([ak[:, cols], av[:, cols]], axis=1).astype(BF16)
    return w


def _fox_weights(w_in, b_f):
    cw = C_HEADS * HEAD_DIM
    q, k, v, f = _split_cols(w_in, [cw, cw, cw, C_HEADS])
    return {"q": q.astype(BF16), "kt": k.T.astype(BF16), "vt": v.T.astype(BF16),
            "f": f.astype(BF16), "ft": f.T.astype(BF16), "bf": b_f.reshape(1, -1), "bft": b_f.reshape(-1, 1)}


NEW_TOKEN_PAD = 128


def _pad_tokens(a):
    return jnp.pad(a, ((0, 0),) * (a.ndim - 1) + ((0, NEW_TOKEN_PAD - a.shape[-1]),))


def kernel(x_prompt, x_sample, cache_a0_kv, cache_a1_kv, cache_a2_kv, state_gla, cache_c_k, cache_c_v, cache_c_logf,
           page_table, c_prompt, c_sample, w_ada, b_ada, g_mix, g_mlp, w_in_ab, w_gla_a2, b_gla_a2, g_gla_out, w_out_ab,
           w_in_fox, b_fox_f, w_out_fox, w_up, w_down, g_final):
    nb, seq, d = x_prompt.shape
    db, t_new, _ = x_sample.shape
    depth = w_ada.shape[0]
    n_pool, page = cache_c_k.shape[1], cache_c_k.shape[2]
    tm_p, tm_s = 512, db * t_new

    ada = _adaln(jnp.concatenate([c_prompt, c_sample], axis=0), w_ada, b_ada)

    def terms(layer):
        tp = [ada[layer, :nb, i * d:(i + 1) * d][:, None, :] for i in range(6)]
        ts = [jnp.repeat(ada[layer, nb:, i * d:(i + 1) * d], t_new, axis=0)[None] for i in range(6)]
        return tp, ts

    xp = x_prompt
    xs = x_sample.reshape(1, db * t_new, d)
    a_caches = (cache_a0_kv, cache_a1_kv, cache_a2_kv)
    a_p, a_s = [[], [], []], [[], [], []]
    gla_p, gla_s = [], []
    ck_p, cv_p, cf_p, ck_s, cv_s, cf_s = [], [], [], [], [], []
    g_final2 = g_final.reshape(1, d)

    for layer in range(depth):
        tp, ts = terms(layer)
        g1 = g_mix[layer].reshape(1, d)
        g2 = g_mlp[layer].reshape(1, d)
        last = layer == depth - 1
        if layer % 2 == 0:
            e = layer // 2
            w = _ab_weights(w_in_ab[e], w_gla_a2[e], b_gla_a2[e])
            w_out_a = w_out_ab[e, :A_BW].astype(BF16)
            w_out_b = w_out_ab[e, A_BW:].astype(BF16)
            g_bo = g_gla_out[e].reshape(1, B_OUT)

            q, kv0, kv1, kv2, bproj, log_a = _ab_in(xp, g1, tp[0], tp[1], w, tm_p)
            kvs = (kv0, kv1, kv2)
            branch = [_dil_prompt(q, kvs[g], g) for g in range(len(A_BRANCHES))]
            o_b, st = _gla(bproj, log_a, jnp.zeros((nb, B_DV, B_QK), F32), g_bo, chunk=64, sub=16)
            mix = [o for o, _ in branch] + [l for _, l in branch]
            xp = _proj_res([o_b], [w_out_a, w_out_b], xp, tp[2], tm_p, mix=mix)
            for g, (win, _) in enumerate(A_BRANCHES):
                keep = min(win, seq)
                a_p[g].append(kvs[g][:, seq - keep:].reshape(nb, keep, 2, A_HEADS, HEAD_DIM))
            gla_p.append(st.reshape(nb, B_DV, B_HEADS, B_DK).transpose(0, 2, 3, 1))

            q, kv0, kv1, kv2, bproj, log_a = _ab_in(xs, g1, ts[0], ts[1], w, tm_s)
            q_s = q.reshape(db, t_new, len(A_BRANCHES), A_HEADS, HEAD_DIM).transpose(0, 2, 3, 1, 4)
            new_t = [a.reshape(db, t_new, 2, A_HEADS, HEAD_DIM).transpose(0, 2, 3, 4, 1) for a in (kv0, kv1, kv2)]
            bufs_t = [c[e].transpose(0, 2, 3, 4, 1) for c in a_caches]
            o_a = _dil_sample(q_s, [_pad_tokens(a) for a in new_t], bufs_t)
            o_a = o_a.transpose(0, 2, 1, 3).reshape(1, db * t_new, A_BW).astype(BF16)
            s0_t = state_gla[e].transpose(0, 3, 1, 2).reshape(db, B_DV, B_QK)
            o_b, st = _gla(bproj.reshape(db, t_new, -1), log_a.reshape(db, t_new, B_QK), s0_t, g_bo,
                           chunk=t_new, sub=16)
            xs = _proj_res([o_a, o_b.reshape(1, db * t_new, B_OUT)], [w_out_a, w_out_b], xs, ts[2], tm_s)
            for g, (win, _) in enumerate(A_BRANCHES):
                full = jnp.concatenate([bufs_t[g], new_t[g]], axis=-1)
                keep = min(win, full.shape[-1])
                a_s[g].append(full[..., full.shape[-1] - keep:].transpose(0, 4, 1, 2, 3))
            gla_s.append(st.reshape(db, B_DV, B_HEADS, B_DK).transpose(0, 2, 3, 1))
        else:
            o_idx = layer // 2
            w = _fox_weights(w_in_fox[o_idx], b_fox_f[o_idx])
            w_out = w_out_fox[o_idx].astype(BF16)
            cw = C_HEADS * HEAD_DIM

            q, kt, vt, ktb, vtb, lft, cumq, cumt = _fox_in(xp, g1, tp[0], tp[1], w, tm_p)
            o = _fox_prompt(q, ktb, vtb, cumq, cumt)
            xp = _proj_res([o], [w_out], xp, tp[2], tm_p)
            ck_p.append(kt.reshape(nb, C_HEADS, HEAD_DIM, seq).transpose(0, 3, 1, 2))
            cv_p.append(vt.reshape(nb, C_HEADS, HEAD_DIM, seq).transpose(0, 3, 1, 2))
            cf_p.append(lft.transpose(0, 2, 1))

            q, kt, vt, _, _, lft, _, _ = _fox_in(xs, g1, ts[0], ts[1], w, tm_s)
            q_s = q.reshape(db, t_new, C_HEADS, HEAD_DIM).transpose(0, 2, 1, 3)
            k_new_t = kt.reshape(C_HEADS, HEAD_DIM, db, t_new).transpose(2, 0, 1, 3)
            v_new_t = vt.reshape(C_HEADS, HEAD_DIM, db, t_new).transpose(2, 0, 1, 3)
            lf_new_t = lft.reshape(C_HEADS, db, t_new).transpose(1, 0, 2)
            o = _fox_sample(q_s, _pad_tokens(k_new_t), _pad_tokens(v_new_t), _pad_tokens(lf_new_t),
                            cache_c_k.transpose(0, 1, 3, 4, 2).reshape(-1, C_HEADS, HEAD_DIM, page),
                            cache_c_v.transpose(0, 1, 3, 4, 2).reshape(-1, C_HEADS, HEAD_DIM, page),
                            cache_c_logf.transpose(0, 1, 3, 2).reshape(-1, C_HEADS, page), page_table, o_idx * n_pool)
            o = o.transpose(0, 2, 1, 3).reshape(1, db * t_new, cw).astype(BF16)
            xs = _proj_res([o], [w_out], xs, ts[2], tm_s)
            ck_s.append(k_new_t.transpose(0, 3, 1, 2))
            cv_s.append(v_new_t.transpose(0, 3, 1, 2))
            cf_s.append(lf_new_t.transpose(0, 2, 1))

        w_up_b = w_up[layer].astype(BF16)
        w_down_b = w_down[layer].astype(BF16)
        xp = _mlp(xp, g2, tp[3], tp[4], tp[5], w_up_b, w_down_b, g_final2, tm_p, last)
        xs = _mlp(xs, g2, ts[3], ts[4], ts[5], w_up_b, w_down_b, g_final2, tm_s, last)

    stack = lambda parts: jnp.stack(parts, axis=0)
    return (xp, xs.reshape(db, t_new, d),
            stack(a_p[0]), stack(a_p[1]), stack(a_p[2]), stack(gla_p),
            stack(ck_p), stack(cv_p), stack(cf_p),
            stack(a_s[0]), stack(a_s[1]), stack(a_s[2]), stack(gla_s),
            stack(ck_s), stack(cv_s), stack(cf_s))
```

```python
import functools

import jax
import jax.numpy as jnp
from jax import lax
from jax.experimental import pallas as pl
from jax.experimental.pallas import tpu as pltpu

F32 = jnp.float32
BF16 = jnp.bfloat16
HIGHEST = lax.Precision.HIGHEST

HEAD_DIM = 64
RMS_EPS = 1e-6
A_BRANCHES = ((128, 1), (512, 4), (2048, 16))
A_HEADS = 8
A_BW = A_HEADS * HEAD_DIM
B_HEADS = 4
B_DK = 64
B_DV = 128
B_QK = B_HEADS * B_DK
B_OUT = B_HEADS * B_DV
B_GATE_RANK = 16
B_GATE_TEMP = 16.0
C_HEADS = 16
QK_SCALE = HEAD_DIM ** -0.5
NEG = -1e30

V7X_VMEM_BYTES = 64 * 1024 * 1024
NT_DIMS = (((1,), (1,)), ((), ()))
TN_DIMS = (((0,), (0,)), ((), ()))


def _params(sem, vmem_mb):
    assert vmem_mb * 1024 * 1024 < V7X_VMEM_BYTES
    return pltpu.CompilerParams(dimension_semantics=sem, vmem_limit_bytes=vmem_mb * 1024 * 1024)


def _iota(shape, dim):
    return lax.broadcasted_iota(jnp.int32, shape, dim)


def _div(x, n):
    assert n & (n - 1) == 0
    return x >> (n.bit_length() - 1)


def _mod(x, n):
    assert n & (n - 1) == 0
    return x & (n - 1)


def _bdot_nn(a, b):
    return lax.dot_general(a, b, (((2,), (1,)), ((0,), (0,))), preferred_element_type=F32)


def _bdot_nt(a, b):
    return lax.dot_general(a, b, (((2,), (2,)), ((0,), (0,))), preferred_element_type=F32)


def _resident(shape):
    nd = len(shape)
    return pl.BlockSpec(shape, lambda *_: (0,) * nd, pipeline_mode=pl.Buffered(1))


def _log_sigmoid(z):
    return jnp.minimum(z, 0.0) - jnp.log1p(jnp.exp(-jnp.abs(z)))


def _modulated(x, g, shift, scale):
    var = jnp.mean(x * x, axis=-1, keepdims=True)
    h = x * lax.rsqrt(var + RMS_EPS) * g
    return h * (1.0 + scale) + shift


def _mod_spec(mod, tm):
    d = mod.shape[-1]
    if mod.shape[1] == 1:
        return pl.BlockSpec((1, 1, d), lambda g, i: (g, 0, 0))
    return pl.BlockSpec((1, tm, d), lambda g, i: (g, i, 0))


def _ada_kernel(c_ref, w_ref, b_ref, o_ref):
    o_ref[0] = jnp.dot(c_ref[...].astype(BF16), w_ref[0].astype(BF16),
                       preferred_element_type=F32) + b_ref[0]


def _adaln(c_all, w_ada, b_ada, tn=1024):
    nl, d, n = w_ada.shape
    nb = c_all.shape[0]
    return pl.pallas_call(
        _ada_kernel,
        grid=(nl, n // tn),
        in_specs=[pl.BlockSpec((nb, d), lambda l, j: (0, 0)),
                  pl.BlockSpec((1, d, tn), lambda l, j: (l, 0, j)),
                  pl.BlockSpec((1, 1, tn), lambda l, j: (l, 0, j))],
        out_specs=pl.BlockSpec((1, nb, tn), lambda l, j: (l, 0, j)),
        out_shape=jax.ShapeDtypeStruct((nl, nb, n), F32),
        compiler_params=_params(("arbitrary", "arbitrary"), 32),
        name="adaln",
    )(c_all, w_ada, b_ada.reshape(nl, 1, n))


def _ab_in_kernel(x_ref, g_ref, sh_ref, sc_ref, wq_ref, wkv0_ref, wkv1_ref, wkv2_ref, wb_ref, wba_ref,
                  wa2_ref, ba2_ref, q_ref, kv0_ref, kv1_ref, kv2_ref, b_ref, la_ref):
    hb = _modulated(x_ref[0], g_ref[...], sh_ref[0], sc_ref[0]).astype(BF16)
    q_ref[0] = (jnp.dot(hb, wq_ref[...], preferred_element_type=F32) * QK_SCALE).astype(q_ref.dtype)
    kv0_ref[0] = jnp.dot(hb, wkv0_ref[...], preferred_element_type=F32)
    kv1_ref[0] = jnp.dot(hb, wkv1_ref[...], preferred_element_type=F32)
    kv2_ref[0] = jnp.dot(hb, wkv2_ref[...], preferred_element_type=F32)
    b_ref[0] = jnp.dot(hb, wb_ref[...], preferred_element_type=F32)
    ba = jnp.dot(hb, wba_ref[...], preferred_element_type=F32)
    z = jnp.dot(ba.astype(BF16), wa2_ref[...], preferred_element_type=F32) + ba2_ref[...]
    la_ref[0] = _log_sigmoid(z) * (1.0 / B_GATE_TEMP)


def _ab_in(x, g, shift, scale, w, tm):
    ng, t, d = x.shape
    widths = [3 * A_BW, 2 * A_BW, 2 * A_BW, 2 * A_BW, 2 * B_QK + 2 * B_OUT, B_QK]
    dtypes = [BF16, F32, F32, F32, F32, F32]
    tok = lambda n: pl.BlockSpec((1, tm, n), lambda gi, i: (gi, i, 0))
    weights = [w["q"], w["kv0"], w["kv1"], w["kv2"], w["b"], w["ba"], w["a2"], w["ba2"]]
    return pl.pallas_call(
        _ab_in_kernel,
        grid=(ng, t // tm),
        in_specs=[tok(d), pl.BlockSpec((1, d), lambda gi, i: (0, 0)), _mod_spec(shift, tm), _mod_spec(scale, tm)]
                 + [_resident(a.shape) for a in weights],
        out_specs=[tok(n) for n in widths],
        out_shape=[jax.ShapeDtypeStruct((ng, t, n), dt) for n, dt in zip(widths, dtypes)],
        compiler_params=_params(("arbitrary", "arbitrary"), 56),
        name="ab_in_proj",
    )(x, g, shift, scale, *weights)


def _dil_prompt_kernel(q_ref, kc_ref, vc_ref, kp_ref, vp_ref, o_ref, lse_ref, *, span):
    n = pl.program_id(2)
    q = q_ref[0]
    kc = kc_ref[0].astype(BF16)
    vc = vc_ref[0].astype(BF16)
    kp = kp_ref[0].astype(BF16)
    vp = vp_ref[0].astype(BF16)
    qi = _iota((span, span), 0)
    kj = _iota((span, span), 1)
    cur_ok = kj <= qi
    prev_ok = kj >= qi
    prev_off = jnp.where(n > 0, 0.0, NEG)
    heads = [slice(h * HEAD_DIM, (h + 1) * HEAD_DIM) for h in range(A_HEADS)]
    scores = [(lax.dot_general(q[:, sl], kc[:, sl], NT_DIMS, preferred_element_type=F32),
               lax.dot_general(q[:, sl], kp[:, sl], NT_DIMS, preferred_element_type=F32)) for sl in heads]
    probs, lses = [], []
    for sc, sp in scores:
        sc = jnp.where(cur_ok, sc, NEG)
        sp = jnp.where(prev_ok, sp + prev_off, NEG)
        m = jnp.maximum(jnp.max(sc, axis=-1, keepdims=True), jnp.max(sp, axis=-1, keepdims=True))
        pc = jnp.exp(sc - m)
        pp = jnp.exp(sp - m)
        den = jnp.sum(pc, axis=-1, keepdims=True) + jnp.sum(pp, axis=-1, keepdims=True)
        inv = 1.0 / den
        probs.append(((pc * inv).astype(BF16), (pp * inv).astype(BF16)))
        lses.append(jnp.broadcast_to(m + jnp.log(den), (span, HEAD_DIM)))
    outs = [jnp.dot(pc, vc[:, sl], preferred_element_type=F32) + jnp.dot(pp, vp[:, sl], preferred_element_type=F32)
            for (pc, pp), sl in zip(probs, heads)]
    o_ref[0] = jnp.concatenate(outs, axis=1)
    lse_ref[0] = jnp.concatenate(lses, axis=1)


def _dil_prompt(q_all, kv, branch):
    win, dil = A_BRANCHES[branch]
    nbatch, s, _ = q_all.shape
    span = win // dil
    rows = s // dil
    assert s % dil == 0 and rows % span == 0
    nblk = rows // span
    q_v = q_all.reshape(nbatch, rows, dil * 3 * A_BW)
    kv_v = kv.reshape(nbatch, rows, dil * 2 * A_BW)
    blk = (1, span, A_BW)
    out_sds = jax.ShapeDtypeStruct((nbatch, rows, dil * A_BW), F32)
    o, lse = pl.pallas_call(
        functools.partial(_dil_prompt_kernel, span=span),
        grid=(nbatch, dil, nblk),
        in_specs=[pl.BlockSpec(blk, lambda b, r, n: (b, n, 3 * r + branch)),
                  pl.BlockSpec(blk, lambda b, r, n: (b, n, 2 * r)),
                  pl.BlockSpec(blk, lambda b, r, n: (b, n, 2 * r + 1)),
                  pl.BlockSpec(blk, lambda b, r, n: (b, jnp.maximum(n - 1, 0), 2 * r)),
                  pl.BlockSpec(blk, lambda b, r, n: (b, jnp.maximum(n - 1, 0), 2 * r + 1))],
        out_specs=[pl.BlockSpec(blk, lambda b, r, n: (b, n, r))] * 2,
        out_shape=[out_sds, out_sds],
        compiler_params=_params(("arbitrary",) * 3, 32),
        name=f"dilated_prompt_{branch}",
    )(q_v, kv_v, kv_v, kv_v, kv_v)
    return o.reshape(nbatch, s, A_BW), lse.reshape(nbatch, s, A_BW)


def _dil_sample_kernel(q_ref, new0_ref, new1_ref, new2_ref, buf0_ref, buf1_ref, buf2_ref, o_ref):
    outs, lses = [], []
    for g, (new_ref, buf_ref) in enumerate(((new0_ref, buf0_ref), (new1_ref, buf1_ref), (new2_ref, buf2_ref))):
        win, dil = A_BRANCHES[g]
        lw = buf_ref.shape[-1]
        npad = new_ref.shape[-1]
        q = q_ref[0, g]
        t_new = q.shape[1]
        s1 = _bdot_nn(q, buf_ref[0, 0].astype(BF16))
        s2 = _bdot_nn(q, new_ref[0, 0].astype(BF16))
        delta1 = lw + _iota(s1.shape, 1) - _iota(s1.shape, 2)
        delta2 = _iota(s2.shape, 1) - _iota(s2.shape, 2)
        s1 = jnp.where((_mod(delta1, dil) == 0) & (delta1 <= win), s1, NEG)
        s2 = jnp.where((_mod(delta2, dil) == 0) & (delta2 >= 0), s2, NEG)
        m = jnp.maximum(jnp.max(s1, axis=-1, keepdims=True), jnp.max(s2, axis=-1, keepdims=True))
        p1 = jnp.exp(s1 - m)
        p2 = jnp.exp(s2 - m)
        den = jnp.sum(p1, axis=-1, keepdims=True) + jnp.sum(p2, axis=-1, keepdims=True)
        inv = 1.0 / den
        o = _bdot_nt((p1 * inv).astype(BF16), buf_ref[0, 1].astype(BF16))
        o = o + _bdot_nt((p2 * inv).astype(BF16), new_ref[0, 1].astype(BF16))
        outs.append(o)
        lses.append(m + jnp.log(den))
    mx = jnp.maximum(jnp.maximum(lses[0], lses[1]), lses[2])
    es = [jnp.exp(l - mx) for l in lses]
    inv = 1.0 / (es[0] + es[1] + es[2])
    o_ref[0] = (es[0] * outs[0] + es[1] * outs[1] + es[2] * outs[2]) * inv


def _dil_sample(q, new_t, bufs_t):
    db, _, nh, t_new, dh = q.shape
    for (win, _), buf in zip(A_BRANCHES, bufs_t):
        assert buf.shape[-1] == win
    per_b = lambda a: pl.BlockSpec((1,) + a.shape[1:], lambda b: (b,) + (0,) * (a.ndim - 1))
    return pl.pallas_call(
        _dil_sample_kernel,
        grid=(db,),
        in_specs=[per_b(q)] + [per_b(a) for a in new_t] + [per_b(a) for a in bufs_t],
        out_specs=pl.BlockSpec((1, nh, t_new, dh), lambda b: (b, 0, 0, 0)),
        out_shape=jax.ShapeDtypeStruct((db, nh, t_new, dh), F32),
        compiler_params=_params(("arbitrary",), 48),
        name="dilated_sample",
    )(q, *new_t, *bufs_t)


def _gla_kernel(qk_ref, v_ref, r_ref, la_ref, s0_ref, gbo_ref, o_ref, st_ref, *, chunk, sub):
    @pl.when(pl.program_id(1) == 0)
    def _():
        st_ref[0] = s0_ref[0]

    q = qk_ref[0, :, :B_QK] * (B_DK ** -0.5)
    k = qk_ref[0, :, B_QK:]
    v = v_ref[0]
    la = la_ref[0]
    rows_in = chunk
    if chunk < sub:
        grow = lambda a: jnp.concatenate([a, jnp.zeros((sub - chunk, a.shape[1]), F32)], axis=0)
        q, k, v, la = grow(q), grow(k), grow(v), grow(la)
        chunk = sub
    tri = (_iota((chunk, chunk), 0) >= _iota((chunk, chunk), 1)).astype(F32)
    cum = jnp.dot(tri, la, precision=HIGHEST, preferred_element_type=F32)
    last = cum[chunk - 1:chunk, :]
    st = st_ref[0]
    qe = (q * jnp.exp(cum)).astype(BF16)
    kd = (k * jnp.exp(last - cum)).astype(BF16)
    vb = v.astype(BF16)
    heads = [(slice(h * B_DK, (h + 1) * B_DK), slice(h * B_DV, (h + 1) * B_DV)) for h in range(B_HEADS)]
    st_b = st.astype(BF16)
    inter = [lax.dot_general(qe[:, ks], st_b[:, ks], NT_DIMS, preferred_element_type=F32) for ks, _ in heads]
    upd = [lax.dot_general(vb[:, vs], kd[:, ks], TN_DIMS, preferred_element_type=F32) for ks, vs in heads]
    starts = list(range(sub, chunk, sub))
    factored = []
    for r0 in starts:
        ref = cum[r0:r0 + 1, :]
        factored.append(((q[r0:r0 + sub] * jnp.exp(cum[r0:r0 + sub] - ref)).astype(BF16),
                         (k[:r0] * jnp.exp(ref - cum[:r0])).astype(BF16)))
    atts = [[lax.dot_general(qt[:, ks], kt[:, ks], NT_DIMS, preferred_element_type=F32).astype(BF16)
             for qt, kt in factored] for ks, _ in heads]
    o_heads = []
    for (_, vs), inter_h, att_h in zip(heads, inter, atts):
        blocks = [jnp.zeros((sub, B_DV), F32)]
        blocks += [jnp.dot(a, vb[:r0, vs], preferred_element_type=F32) for a, r0 in zip(att_h, starts)]
        o_heads.append(inter_h + jnp.concatenate(blocks, axis=0) if starts else inter_h)
    st_ref[0] = st * jnp.exp(last) + jnp.concatenate(upd, axis=1)
    o = jnp.concatenate(o_heads, axis=1)

    spread = (_div(_iota((B_QK, B_OUT), 0), B_DK) == _div(_iota((B_QK, B_OUT), 1), B_DV)).astype(BF16)
    row_in_block = _mod(_iota((chunk, B_QK), 0), sub)
    for off in range(sub):
        k_s = k if off == 0 else pltpu.roll(k, off, 0)
        c_s = cum if off == 0 else pltpu.roll(cum, off, 0)
        v_s = v if off == 0 else pltpu.roll(v, off, 0)
        term = q * k_s * jnp.exp(jnp.minimum(cum - c_s, 0.0))
        term = jnp.where(row_in_block >= off, term, 0.0)
        att = jnp.dot(term.astype(BF16), spread, preferred_element_type=F32)
        o = o + att * v_s

    o = o[:rows_in]
    r = r_ref[0]
    gate = r * (1.0 / (1.0 + jnp.exp(-r)))
    normed = []
    for h in range(B_HEADS):
        vs = slice(h * B_DV, (h + 1) * B_DV)
        o_h = o[:, vs]
        var = jnp.mean(o_h * o_h, axis=-1, keepdims=True)
        normed.append(o_h * lax.rsqrt(var + RMS_EPS))
    o_ref[0] = (jnp.concatenate(normed, axis=1) * gbo_ref[...] * gate).astype(o_ref.dtype)


def _gla(bproj, log_a, s0_t, g_bo, chunk, sub):
    nb, t, _ = bproj.shape
    assert t % chunk == 0 and (chunk % sub == 0 or (chunk < sub and t == chunk))
    half = 2 * B_QK
    assert half == B_OUT
    col = lambda j: pl.BlockSpec((1, chunk, half), lambda b, c: (b, c, j))
    st_spec = pl.BlockSpec((1, B_DV, B_QK), lambda b, c: (b, 0, 0))
    return pl.pallas_call(
        functools.partial(_gla_kernel, chunk=chunk, sub=sub),
        grid=(nb, t // chunk),
        in_specs=[col(0), col(1), col(2),
                  pl.BlockSpec((1, chunk, B_QK), lambda b, c: (b, c, 0)),
                  st_spec,
                  pl.BlockSpec((1, B_OUT), lambda b, c: (0, 0))],
        out_specs=[pl.BlockSpec((1, chunk, B_OUT), lambda b, c: (b, c, 0)), st_spec],
        out_shape=[jax.ShapeDtypeStruct((nb, t, B_OUT), BF16), jax.ShapeDtypeStruct((nb, B_DV, B_QK), F32)],
        compiler_params=_params(("arbitrary", "arbitrary"), 32),
        name=f"gla_chunk{chunk}",
    )(bproj, bproj, bproj, log_a, s0_t, g_bo)


def _proj_res_kernel(*refs, n_in, mix3):
    if mix3:
        o0, o1, o2, l0, l1, l2 = (r[0] for r in refs[:6])
        refs = refs[6:]
        mx = jnp.maximum(jnp.maximum(l0, l1), l2)
        e0, e1, e2 = jnp.exp(l0 - mx), jnp.exp(l1 - mx), jnp.exp(l2 - mx)
        first = ((e0 * o0 + e1 * o1 + e2 * o2) * (1.0 / (e0 + e1 + e2))).astype(BF16)
        ins = [first] + [r[0] for r in refs[:n_in - 1]]
        refs = refs[n_in - 1:]
    else:
        ins = [r[0] for r in refs[:n_in]]
        refs = refs[n_in:]
    w_refs, (x_ref, gate_ref, out_ref) = refs[:n_in], refs[n_in:]
    y = jnp.dot(ins[0], w_refs[0][...], preferred_element_type=F32)
    for a, w_ref in zip(ins[1:], w_refs[1:]):
        y = y + jnp.dot(a, w_ref[...], preferred_element_type=F32)
    out_ref[0] = x_ref[0] + gate_ref[0] * y


def _proj_res(ins, ws, x, gate, tm, mix=None):
    ng, t, d = x.shape
    tok = lambda a: pl.BlockSpec((1, tm, a.shape[-1]), lambda gi, i: (gi, i, 0))
    lead = list(mix) if mix is not None else []
    n_in = len(ws)
    return pl.pallas_call(
        functools.partial(_proj_res_kernel, n_in=n_in, mix3=mix is not None),
        grid=(ng, t // tm),
        in_specs=[tok(a) for a in lead] + [tok(a) for a in ins] + [_resident(w.shape) for w in ws]
                 + [tok(x), _mod_spec(gate, tm)],
        out_specs=tok(x),
        out_shape=jax.ShapeDtypeStruct(x.shape, F32),
        compiler_params=_params(("arbitrary", "arbitrary"), 48),
        name="proj_residual_mix" if mix is not None else "proj_residual",
    )(*lead, *ins, *ws, x, gate)


def _mlp_kernel(x_ref, g_ref, sh_ref, sc_ref, gate_ref, wu_ref, wd_ref, gf_ref, out_ref, *, fchunk, final_norm):
    x = x_ref[0]
    hb = _modulated(x, g_ref[...], sh_ref[0], sc_ref[0]).astype(BF16)
    acc = jnp.zeros(x.shape, F32)
    for f0 in range(0, wu_ref.shape[1], fchunk):
        u = jnp.maximum(jnp.dot(hb, wu_ref[:, f0:f0 + fchunk], preferred_element_type=F32), 0.0)
        acc = acc + jnp.dot((u * u).astype(BF16), wd_ref[f0:f0 + fchunk, :], preferred_element_type=F32)
    y = x + gate_ref[0] * acc
    if final_norm:
        var = jnp.mean(y * y, axis=-1, keepdims=True)
        y = y * lax.rsqrt(var + RMS_EPS) * gf_ref[...]
    out_ref[0] = y


def _mlp(x, g, shift, scale, gate, w_up, w_down, g_final, tm, final_norm):
    ng, t, d = x.shape
    tok = pl.BlockSpec((1, tm, d), lambda gi, i: (gi, i, 0))
    vec = pl.BlockSpec((1, d), lambda gi, i: (0, 0))
    return pl.pallas_call(
        functools.partial(_mlp_kernel, fchunk=1024, final_norm=final_norm),
        grid=(ng, t // tm),
        in_specs=[tok, vec, _mod_spec(shift, tm), _mod_spec(scale, tm), _mod_spec(gate, tm),
                  _resident(w_up.shape), _resident(w_down.shape), vec],
        out_specs=tok,
        out_shape=jax.ShapeDtypeStruct(x.shape, F32),
        compiler_params=_params(("arbitrary", "arbitrary"), 56),
        name="mlp_final" if final_norm else "mlp",
    )(x, g, shift, scale, gate, w_up, w_down, g_final)


def _bf16_parts(c):
    c1 = c.astype(BF16).astype(F32)
    c2 = (c - c1).astype(BF16).astype(F32)
    c3 = (c - c1 - c2).astype(BF16).astype(F32)
    return c1, c2, c3


def _fox_in_kernel(x_ref, g_ref, sh_ref, sc_ref, wq_ref, wkt_ref, wvt_ref, wf_ref, wft_ref, bf_ref, bft_ref,
                   selq_ref, oneq_ref, selk_ref, onek_ref,
                   qa_ref, kt_ref, vt_ref, kta_ref, vta_ref, lft_ref, carry_row_ref, carry_col_ref, *, tm):
    @pl.when(pl.program_id(1) == 0)
    def _():
        carry_row_ref[...] = jnp.zeros_like(carry_row_ref)
        carry_col_ref[...] = jnp.zeros_like(carry_col_ref)

    dh = HEAD_DIM
    hb = _modulated(x_ref[0], g_ref[...], sh_ref[0], sc_ref[0]).astype(BF16)
    q = jnp.dot(hb, wq_ref[...], preferred_element_type=F32) * QK_SCALE
    kt = lax.dot_general(wkt_ref[...], hb, NT_DIMS, preferred_element_type=F32)
    vt = lax.dot_general(wvt_ref[...], hb, NT_DIMS, preferred_element_type=F32)
    kt_ref[0] = kt
    vt_ref[0] = vt
    lf = _log_sigmoid(jnp.dot(hb, wf_ref[...], preferred_element_type=F32) + bf_ref[...])
    lft = _log_sigmoid(lax.dot_general(wft_ref[...], hb, NT_DIMS, preferred_element_type=F32) + bft_ref[...])
    lft_ref[0] = lft
    upto = (_iota((tm, tm), 0) <= _iota((tm, tm), 1)).astype(F32)
    since = (_iota((tm, tm), 0) >= _iota((tm, tm), 1)).astype(F32)
    cumq = jnp.dot(since, lf, precision=HIGHEST, preferred_element_type=F32) + carry_row_ref[...]
    carry_row_ref[...] = cumq[tm - 1:tm, :]
    cumt = jnp.dot(lft, upto, precision=HIGHEST, preferred_element_type=F32) + carry_col_ref[:, :1]
    carry_col_ref[...] = jnp.broadcast_to(cumt[:, tm - 1:tm], carry_col_ref.shape)

    parts_q = jnp.concatenate(_bf16_parts(cumq), axis=1).astype(BF16)
    bias_q = jnp.dot(parts_q, selq_ref[...], preferred_element_type=F32) + oneq_ref[...]
    parts_k = jnp.concatenate(_bf16_parts(cumt), axis=0).astype(BF16)
    bias_k = jnp.dot(selk_ref[...], parts_k, preferred_element_type=F32) + onek_ref[...]
    ones = jnp.ones((dh, tm), F32)
    qa, kta, vta = [], [], []
    for h in range(C_HEADS):
        sl = slice(h * dh, (h + 1) * dh)
        qa += [q[:, sl], bias_q[:, sl]]
        kta += [kt[sl], bias_k[sl]]
        vta += [vt[sl], ones]
    qa_ref[0] = jnp.concatenate(qa, axis=1).astype(BF16)
    kta_ref[0] = jnp.concatenate(kta, axis=0).astype(BF16)
    vta_ref[0] = jnp.concatenate(vta, axis=0).astype(BF16)


def _fox_in(x, g, shift, scale, w, tm):
    ng, t, d = x.shape
    cw = C_HEADS * HEAD_DIM
    tok = lambda n: pl.BlockSpec((1, tm, n), lambda gi, i: (gi, i, 0))
    tok_t = lambda n: pl.BlockSpec((1, n, tm), lambda gi, i: (gi, 0, i))
    weights = [w["q"], w["kt"], w["vt"], w["f"], w["ft"], w["bf"], w["bft"], w["selq"], w["oneq"], w["selk"], w["onek"]]
    sds = jax.ShapeDtypeStruct
    return pl.pallas_call(
        functools.partial(_fox_in_kernel, tm=tm),
        grid=(ng, t // tm),
        in_specs=[tok(d), pl.BlockSpec((1, d), lambda gi, i: (0, 0)), _mod_spec(shift, tm), _mod_spec(scale, tm)]
                 + [_resident(a.shape) for a in weights],
        out_specs=[tok(2 * cw), tok_t(cw), tok_t(cw), tok_t(2 * cw), tok_t(2 * cw), tok_t(C_HEADS)],
        out_shape=[sds((ng, t, 2 * cw), BF16), sds((ng, cw, t), F32), sds((ng, cw, t), F32),
                   sds((ng, 2 * cw, t), BF16), sds((ng, 2 * cw, t), BF16), sds((ng, C_HEADS, t), F32)],
        scratch_shapes=[pltpu.VMEM((1, C_HEADS), F32), pltpu.VMEM((C_HEADS, 128), F32)],
        compiler_params=_params(("arbitrary", "arbitrary"), 56),
        name="fox_in_proj",
    )(x, g, shift, scale, *weights)


FOX_TQ = 512
FOX_HEAD_GROUP = 4


def _fox_prompt_kernel(q_ref, kt_ref, vt_ref, o_ref, m_ref, acc_ref):
    qi = pl.program_id(1)
    ki = pl.program_id(2)
    tq = FOX_TQ
    hw = 2 * HEAD_DIM

    @pl.when(ki == 0)
    def _():
        m_ref[...] = jnp.full(m_ref.shape, NEG, F32)
        acc_ref[...] = jnp.zeros_like(acc_ref)

    def update(diagonal):
        causal = _iota((tq, tq), 0) >= _iota((tq, tq), 1)
        for h0 in range(0, C_HEADS, FOX_HEAD_GROUP):
            group = range(h0, h0 + FOX_HEAD_GROUP)
            lanes = [slice(h * hw, (h + 1) * hw) for h in group]
            scores = [jnp.dot(q_ref[0, :, sl], kt_ref[0, sl, :], preferred_element_type=F32) for sl in lanes]
            probs, corrs = [], []
            for h, s in zip(group, scores):
                if diagonal:
                    s = jnp.where(causal, s, NEG)
                m_prev = m_ref[h]
                m_new = jnp.maximum(m_prev, jnp.max(s, axis=-1, keepdims=True))
                m_ref[h] = m_new
                corrs.append(jnp.exp(m_prev - m_new))
                probs.append(jnp.exp(s - jnp.concatenate([m_new] * (tq // hw), axis=1)).astype(BF16))
            pvs = [lax.dot_general(p, vt_ref[0, sl, :], NT_DIMS, preferred_element_type=F32)
                   for p, sl in zip(probs, lanes)]
            accs = [acc_ref[:, sl] * corr + pv for sl, corr, pv in zip(lanes, corrs, pvs)]
            if diagonal:
                outs = [acc * (1.0 / pltpu.roll(acc, HEAD_DIM, 1)) for acc in accs]
                for j in range(0, FOX_HEAD_GROUP, 2):
                    pair = jnp.concatenate([outs[j][:, :HEAD_DIM], outs[j + 1][:, :HEAD_DIM]], axis=1)
                    o_ref[0, :, (h0 + j) * HEAD_DIM:(h0 + j + 2) * HEAD_DIM] = pair.astype(o_ref.dtype)
            else:
                for sl, acc in zip(lanes, accs):
                    acc_ref[:, sl] = acc

    @pl.when(ki < qi)
    def _():
        update(False)

    @pl.when(ki == qi)
    def _():
        update(True)


def _fox_prompt(q_aug, kt_aug, vt_aug):
    nb, s, w2 = q_aug.shape
    tq = FOX_TQ
    nq = s // tq
    assert 2 * HEAD_DIM == 128 and FOX_HEAD_GROUP % 2 == 0 and C_HEADS % FOX_HEAD_GROUP == 0
    kv_spec = pl.BlockSpec((1, w2, tq), lambda b, i, j: (b, 0, jnp.minimum(i, j)))
    return pl.pallas_call(
        _fox_prompt_kernel,
        grid=(nb, nq, nq),
        in_specs=[pl.BlockSpec((1, tq, w2), lambda b, i, j: (b, i, 0)), kv_spec, kv_spec],
        out_specs=pl.BlockSpec((1, tq, w2 // 2), lambda b, i, j: (b, i, 0)),
        out_shape=jax.ShapeDtypeStruct((nb, s, w2 // 2), BF16),
        scratch_shapes=[pltpu.VMEM((C_HEADS, tq, 2 * HEAD_DIM), F32), pltpu.VMEM((tq, w2), F32)],
        compiler_params=_params(("arbitrary",) * 3, 48),
        name="fox_prompt",
    )(q_aug, kt_aug, vt_aug)


def _rows_as_batch(x):
    return jnp.concatenate([x[h:h + 1][None] for h in range(x.shape[0])], axis=0)


def _fox_sample_kernel(pt_ref, q_ref, kn_ref, vn_ref, lfn_ref, *refs, n_pp, n_steps):
    del pt_ref
    k_refs, v_refs, lf_refs = refs[:n_pp], refs[n_pp:2 * n_pp], refs[2 * n_pp:3 * n_pp]
    o_ref, gq_ref, carry_ref, m_ref, l_ref, acc_ref = refs[3 * n_pp:]
    step = pl.program_id(1)
    q = q_ref[0]
    nh = q.shape[0]
    page = lf_refs[0].shape[-1]
    npad = lfn_ref.shape[-1]

    def new_token_cumsum():
        upto = (_iota((npad, npad), 0) <= _iota((npad, npad), 1)).astype(F32)
        return _rows_as_batch(jnp.dot(lfn_ref[0], upto, precision=HIGHEST, preferred_element_type=F32))

    def absorb(scores, values):
        m_prev = m_ref[...]
        m_new = m_prev
        for s in scores:
            m_new = jnp.maximum(m_new, jnp.max(s, axis=-1, keepdims=True))
        corr = jnp.exp(m_prev - m_new)
        l_new = l_ref[...] * corr
        acc = acc_ref[...] * corr
        for s, v_t in zip(scores, values):
            p = jnp.exp(s - m_new)
            l_new = l_new + jnp.sum(p, axis=-1, keepdims=True)
            acc = acc + _bdot_nt(p.astype(BF16), v_t)
        l_ref[...] = l_new
        acc_ref[...] = acc
        m_ref[...] = m_new

    @pl.when(step == 0)
    def _():
        g = jnp.broadcast_to(new_token_cumsum(), (nh, q.shape[1], npad))
        own_t = _iota(g.shape, 1) == _iota(g.shape, 2)
        gq_ref[...] = jnp.sum(jnp.where(own_t, g, 0.0), axis=-1, keepdims=True)
        carry_ref[...] = jnp.zeros_like(carry_ref)
        m_ref[...] = jnp.full(m_ref.shape, NEG, F32)
        l_ref[...] = jnp.zeros_like(l_ref)
        acc_ref[...] = jnp.zeros_like(acc_ref)

    lf_all = jnp.concatenate([r[0] for r in lf_refs], axis=0)
    later = jnp.where(_iota((page, 2 * page), 1) < page,
                      (_iota((page, 2 * page), 0) > _iota((page, 2 * page), 1)).astype(F32), 1.0)
    sums = jnp.dot(lf_all, later, precision=HIGHEST, preferred_element_type=F32)
    carry = carry_ref[...]
    scores = []
    for j in reversed(range(n_pp)):
        rows = slice(j * nh, (j + 1) * nh)
        bias = sums[rows, :page] + carry
        carry = carry + sums[rows, page:]
        scores.append(_bdot_nn(q, k_refs[j][0].astype(BF16)) + _rows_as_batch(bias) + gq_ref[...])
    carry_ref[...] = carry
    absorb(scores, [v_refs[j][0].astype(BF16) for j in reversed(range(n_pp))])

    @pl.when(step == n_steps - 1)
    def _():
        s = _bdot_nn(q, kn_ref[0].astype(BF16)) + gq_ref[...] - new_token_cumsum()
        causal = _iota(s.shape, 1) >= _iota(s.shape, 2)
        absorb([jnp.where(causal, s, NEG)], [vn_ref[0].astype(BF16)])
        o_ref[0] = acc_ref[...] * (1.0 / l_ref[...])


FOX_PAGES_PER_STEP = 8


def _fox_sample(q, k_new_t, v_new_t, lf_new_t, cache_k_t, cache_v_t, cache_lf_t, page_table, layer_off):
    db, nh, t_new, dh = q.shape
    n_pages = page_table.shape[1]
    page = cache_k_t.shape[-1]
    n_pp = FOX_PAGES_PER_STEP
    assert n_pages % n_pp == 0
    n_steps = n_pages // n_pp
    per_b = lambda a: pl.BlockSpec((1,) + a.shape[1:], lambda b, i, pt: (b,) + (0,) * (a.ndim - 1))

    def paged(a, j):
        return pl.BlockSpec((1,) + a.shape[1:],
                            lambda b, i, pt: (layer_off + pt[b, n_pages - n_pp * (i + 1) + j],) + (0,) * (a.ndim - 1))

    caches = (cache_k_t, cache_v_t, cache_lf_t)
    return pl.pallas_call(
        functools.partial(_fox_sample_kernel, n_pp=n_pp, n_steps=n_steps),
        grid_spec=pltpu.PrefetchScalarGridSpec(
            num_scalar_prefetch=1,
            grid=(db, n_steps),
            in_specs=[per_b(q), per_b(k_new_t), per_b(v_new_t), per_b(lf_new_t)]
                     + [paged(a, j) for a in caches for j in range(n_pp)],
            out_specs=pl.BlockSpec((1, nh, t_new, dh), lambda b, i, pt: (b, 0, 0, 0)),
            scratch_shapes=[pltpu.VMEM((nh, t_new, 1), F32), pltpu.VMEM((nh, page), F32),
                            pltpu.VMEM((nh, t_new, 1), F32), pltpu.VMEM((nh, t_new, 1), F32),
                            pltpu.VMEM((nh, t_new, dh), F32)]),
        out_shape=jax.ShapeDtypeStruct((db, nh, t_new, dh), F32),
        compiler_params=_params(("arbitrary", "arbitrary"), 40),
        name="fox_sample",
    )(page_table, q, k_new_t, v_new_t, lf_new_t, *[a for a in caches for _ in range(n_pp)])


def _split_cols(w, sizes):
    out, a = [], 0
    for s in sizes:
        out.append(w[:, a:a + s])
        a += s
    return out


def _ab_weights(w_in, w_a2, b_a2):
    aw = len(A_BRANCHES) * A_BW
    aq, ak, av, bq, bk, bv, br, ba = _split_cols(w_in, [aw, aw, aw, B_QK, B_QK, B_OUT, B_OUT, B_GATE_RANK])
    w = {"q": aq.astype(BF16), "b": jnp.concatenate([bq, bk, bv, br], axis=1).astype(BF16),
         "ba": ba.astype(BF16), "a2": w_a2.astype(BF16), "ba2": b_a2.reshape(1, -1)}
    for g in range(len(A_BRANCHES)):
        cols = slice(g * A_BW, (g + 1) * A_BW)
        w[f"kv{g}"] = jnp.concatenate([ak[:, cols], av[:, cols]], axis=1).astype(BF16)
    return w


def _fox_weights(w_in, b_f):
    cw = C_HEADS * HEAD_DIM
    q, k, v, f = _split_cols(w_in, [cw, cw, cw, C_HEADS])
    nparts = 3
    head = jnp.arange(C_HEADS)
    selq = jnp.zeros((nparts * C_HEADS, cw), F32)
    selk = jnp.zeros((cw, nparts * C_HEADS), F32)
    oneq = jnp.zeros((1, cw), F32)
    onek = jnp.zeros((cw, 1), F32)
    for j in range(nparts):
        selq = selq.at[j * C_HEADS + head, head * HEAD_DIM + nparts + j].set(1.0)
        selk = selk.at[head * HEAD_DIM + j, j * C_HEADS + head].set(-1.0)
        oneq = oneq.at[0, head * HEAD_DIM + j].set(1.0)
        onek = onek.at[head * HEAD_DIM + nparts + j, 0].set(1.0)
    return {"q": q.astype(BF16), "kt": k.T.astype(BF16), "vt": v.T.astype(BF16),
            "f": f.astype(BF16), "ft": f.T.astype(BF16), "bf": b_f.reshape(1, -1), "bft": b_f.reshape(-1, 1),
            "selq": selq.astype(BF16), "oneq": oneq, "selk": selk.astype(BF16), "onek": onek}


NEW_TOKEN_PAD = 128


def _pad_tokens(a):
    return jnp.pad(a, ((0, 0),) * (a.ndim - 1) + ((0, NEW_TOKEN_PAD - a.shape[-1]),))


def kernel(x_prompt, x_sample, cache_a0_kv, cache_a1_kv, cache_a2_kv, state_gla, cache_c_k, cache_c_v, cache_c_logf,
           page_table, c_prompt, c_sample, w_ada, b_ada, g_mix, g_mlp, w_in_ab, w_gla_a2, b_gla_a2, g_gla_out, w_out_ab,
           w_in_fox, b_fox_f, w_out_fox, w_up, w_down, g_final):
    nb, seq, d = x_prompt.shape
    db, t_new, _ = x_sample.shape
    depth = w_ada.shape[0]
    n_pool, page = cache_c_k.shape[1], cache_c_k.shape[2]
    tm_p, tm_s = 512, db * t_new

    ada = _adaln(jnp.concatenate([c_prompt, c_sample], axis=0), w_ada, b_ada)

    def terms(layer):
        tp = [ada[layer, :nb, i * d:(i + 1) * d][:, None, :] for i in range(6)]
        ts = [jnp.repeat(ada[layer, nb:, i * d:(i + 1) * d], t_new, axis=0)[None] for i in range(6)]
        return tp, ts

    xp = x_prompt
    xs = x_sample.reshape(1, db * t_new, d)
    a_caches = (cache_a0_kv, cache_a1_kv, cache_a2_kv)
    a_p, a_s = [[], [], []], [[], [], []]
    gla_p, gla_s = [], []
    ck_p, cv_p, cf_p, ck_s, cv_s, cf_s = [], [], [], [], [], []
    g_final2 = g_final.reshape(1, d)

    for layer in range(depth):
        tp, ts = terms(layer)
        g1 = g_mix[layer].reshape(1, d)
        g2 = g_mlp[layer].reshape(1, d)
        last = layer == depth - 1
        if layer % 2 == 0:
            e = layer // 2
            w = _ab_weights(w_in_ab[e], w_gla_a2[e], b_gla_a2[e])
            w_out_a = w_out_ab[e, :A_BW].astype(BF16)
            w_out_b = w_out_ab[e, A_BW:].astype(BF16)
            g_bo = g_gla_out[e].reshape(1, B_OUT)

            q, kv0, kv1, kv2, bproj, log_a = _ab_in(xp, g1, tp[0], tp[1], w, tm_p)
            kvs = (kv0, kv1, kv2)
            branch = [_dil_prompt(q, kvs[g], g) for g in range(len(A_BRANCHES))]
            o_b, st = _gla(bproj, log_a, jnp.zeros((nb, B_DV, B_QK), F32), g_bo, chunk=128, sub=8)
            mix = [o for o, _ in branch] + [l for _, l in branch]
            xp = _proj_res([o_b], [w_out_a, w_out_b], xp, tp[2], tm_p, mix=mix)
            for g, (win, _) in enumerate(A_BRANCHES):
                keep = min(win, seq)
                a_p[g].append(kvs[g][:, seq - keep:].reshape(nb, keep, 2, A_HEADS, HEAD_DIM))
            gla_p.append(st.reshape(nb, B_DV, B_HEADS, B_DK).transpose(0, 2, 3, 1))

            q, kv0, kv1, kv2, bproj, log_a = _ab_in(xs, g1, ts[0], ts[1], w, tm_s)
            q_s = q.reshape(db, t_new, len(A_BRANCHES), A_HEADS, HEAD_DIM).transpose(0, 2, 3, 1, 4)
            new_t = [a.reshape(db, t_new, 2, A_HEADS, HEAD_DIM).transpose(0, 2, 3, 4, 1) for a in (kv0, kv1, kv2)]
            bufs_t = [c[e].transpose(0, 2, 3, 4, 1) for c in a_caches]
            o_a = _dil_sample(q_s, [_pad_tokens(a) for a in new_t], bufs_t)
            o_a = o_a.transpose(0, 2, 1, 3).reshape(1, db * t_new, A_BW).astype(BF16)
            s0_t = state_gla[e].transpose(0, 3, 1, 2).reshape(db, B_DV, B_QK)
            o_b, st = _gla(bproj.reshape(db, t_new, -1), log_a.reshape(db, t_new, B_QK), s0_t, g_bo,
                           chunk=t_new, sub=16)
            xs = _proj_res([o_a, o_b.reshape(1, db * t_new, B_OUT)], [w_out_a, w_out_b], xs, ts[2], tm_s)
            for g, (win, _) in enumerate(A_BRANCHES):
                full = jnp.concatenate([bufs_t[g], new_t[g]], axis=-1)
                keep = min(win, full.shape[-1])
                a_s[g].append(full[..., full.shape[-1] - keep:].transpose(0, 4, 1, 2, 3))
            gla_s.append(st.reshape(db, B_DV, B_HEADS, B_DK).transpose(0, 2, 3, 1))
        else:
            o_idx = layer // 2
            w = _fox_weights(w_in_fox[o_idx], b_fox_f[o_idx])
            w_out = w_out_fox[o_idx].astype(BF16)
            cw = C_HEADS * HEAD_DIM

            q_aug, kt, vt, kt_aug, vt_aug, lft = _fox_in(xp, g1, tp[0], tp[1], w, tm_p)
            o = _fox_prompt(q_aug, kt_aug, vt_aug)
            xp = _proj_res([o], [w_out], xp, tp[2], tm_p)
            ck_p.append(kt.reshape(nb, C_HEADS, HEAD_DIM, seq).transpose(0, 3, 1, 2))
            cv_p.append(vt.reshape(nb, C_HEADS, HEAD_DIM, seq).transpose(0, 3, 1, 2))
            cf_p.append(lft.transpose(0, 2, 1))

            q_aug, kt, vt, _, _, lft = _fox_in(xs, g1, ts[0], ts[1], w, tm_s)
            q_s = q_aug.reshape(db, t_new, C_HEADS, 2 * HEAD_DIM)[..., :HEAD_DIM].transpose(0, 2, 1, 3)
            k_new_t = kt.reshape(C_HEADS, HEAD_DIM, db, t_new).transpose(2, 0, 1, 3)
            v_new_t = vt.reshape(C_HEADS, HEAD_DIM, db, t_new).transpose(2, 0, 1, 3)
            lf_new_t = lft.reshape(C_HEADS, db, t_new).transpose(1, 0, 2)
            o = _fox_sample(q_s, _pad_tokens(k_new_t), _pad_tokens(v_new_t), _pad_tokens(lf_new_t),
                            cache_c_k.transpose(0, 1, 3, 4, 2).reshape(-1, C_HEADS, HEAD_DIM, page),
                            cache_c_v.transpose(0, 1, 3, 4, 2).reshape(-1, C_HEADS, HEAD_DIM, page),
                            cache_c_logf.transpose(0, 1, 3, 2).reshape(-1, C_HEADS, page), page_table, o_idx * n_pool)
            o = o.transpose(0, 2, 1, 3).reshape(1, db * t_new, cw).astype(BF16)
            xs = _proj_res([o], [w_out], xs, ts[2], tm_s)
            ck_s.append(k_new_t.transpose(0, 3, 1, 2))
            cv_s.append(v_new_t.transpose(0, 3, 1, 2))
            cf_s.append(lf_new_t.transpose(0, 2, 1))

        w_up_b = w_up[layer].astype(BF16)
        w_down_b = w_down[layer].astype(BF16)
        xp = _mlp(xp, g2, tp[3], tp[4], tp[5], w_up_b, w_down_b, g_final2, tm_p, last)
        xs = _mlp(xs, g2, ts[3], ts[4], ts[5], w_up_b, w_down_b, g_final2, tm_s, last)

    stack = lambda parts: jnp.stack(parts, axis=0)
    return (xp, xs.reshape(db, t_new, d),
            stack(a_p[0]), stack(a_p[1]), stack(a_p[2]), stack(gla_p),
            stack(ck_p), stack(cv_p), stack(cf_p),
            stack(a_s[0]), stack(a_s[1]), stack(a_s[2]), stack(gla_s),
            stack(ck_s), stack(cv_s), stack(cf_s))
```

```python
import functools

import jax
import jax.numpy as jnp
from jax import lax
from jax.experimental import pallas as pl
from jax.experimental.pallas import tpu as pltpu

F32 = jnp.float32
BF16 = jnp.bfloat16
HIGHEST = lax.Precision.HIGHEST

HEAD_DIM = 64
RMS_EPS = 1e-6
A_BRANCHES = ((128, 1), (512, 4), (2048, 16))
A_HEADS = 8
A_BW = A_HEADS * HEAD_DIM
B_HEADS = 4
B_DK = 64
B_DV = 128
B_QK = B_HEADS * B_DK
B_OUT = B_HEADS * B_DV
B_GATE_RANK = 16
B_GATE_TEMP = 16.0
C_HEADS = 16
QK_SCALE = HEAD_DIM ** -0.5
NEG = -1e30

V7X_VMEM_BYTES = 64 * 1024 * 1024
LANES = 128
NT_DIMS = (((1,), (1,)), ((), ()))
TN_DIMS = (((0,), (0,)), ((), ()))


def _params(sem, vmem_mb):
    assert vmem_mb * 1024 * 1024 < V7X_VMEM_BYTES
    return pltpu.CompilerParams(dimension_semantics=sem, vmem_limit_bytes=vmem_mb * 1024 * 1024)


def _iota(shape, dim):
    return lax.broadcasted_iota(jnp.int32, shape, dim)


def _div(x, n):
    assert n & (n - 1) == 0
    return x >> (n.bit_length() - 1)


def _mod(x, n):
    assert n & (n - 1) == 0
    return x & (n - 1)


def _bdot_nn(a, b):
    return lax.dot_general(a, b, (((2,), (1,)), ((0,), (0,))), preferred_element_type=F32)


def _bdot_nt(a, b):
    return lax.dot_general(a, b, (((2,), (2,)), ((0,), (0,))), preferred_element_type=F32)


def _resident(shape):
    nd = len(shape)
    return pl.BlockSpec(shape, lambda *_: (0,) * nd, pipeline_mode=pl.Buffered(1))


def _log_sigmoid(z):
    return jnp.minimum(z, 0.0) - jnp.log1p(jnp.exp(-jnp.abs(z)))


def _modulated(x, g, shift, scale):
    var = jnp.mean(x * x, axis=-1, keepdims=True)
    h = x * lax.rsqrt(var + RMS_EPS) * g
    return h * (1.0 + scale) + shift


def _mod_spec(mod, tm):
    d = mod.shape[-1]
    if mod.shape[1] == 1:
        return pl.BlockSpec((1, 1, d), lambda g, i: (g, 0, 0))
    return pl.BlockSpec((1, tm, d), lambda g, i: (g, i, 0))


def _ada_kernel(c_ref, w_ref, b_ref, o_ref):
    o_ref[0] = jnp.dot(c_ref[...].astype(BF16), w_ref[0].astype(BF16),
                       preferred_element_type=F32) + b_ref[0]


def _adaln(c_all, w_ada, b_ada, tn=1024):
    nl, d, n = w_ada.shape
    nb = c_all.shape[0]
    return pl.pallas_call(
        _ada_kernel,
        grid=(nl, n // tn),
        in_specs=[pl.BlockSpec((nb, d), lambda l, j: (0, 0)),
                  pl.BlockSpec((1, d, tn), lambda l, j: (l, 0, j)),
                  pl.BlockSpec((1, 1, tn), lambda l, j: (l, 0, j))],
        out_specs=pl.BlockSpec((1, nb, tn), lambda l, j: (l, 0, j)),
        out_shape=jax.ShapeDtypeStruct((nl, nb, n), F32),
        compiler_params=_params(("arbitrary", "arbitrary"), 32),
        name="adaln",
    )(c_all, w_ada, b_ada.reshape(nl, 1, n))


def _ab_in_kernel(x_ref, g_ref, sh_ref, sc_ref, wq_ref, wkv0_ref, wkv1_ref, wkv2_ref, wb_ref, wba_ref,
                  wa2_ref, ba2_ref, q_ref, qv1_ref, qv2_ref, kv0_ref, kvv1_ref, kv2_ref, kvv2_ref, b_ref, la_ref,
                  stage_ref, *, tm):
    def emit_view(val, view_ref, dil):
        width = val.shape[1]
        for c in range(width // LANES):
            stage_ref[c] = val[:, c * LANES:(c + 1) * LANES]
        for r in range(dil):
            for c in range(width // LANES):
                rows = stage_ref[c, pl.ds(r, tm // dil, stride=dil), :]
                view_ref[0, :, r * width + c * LANES:r * width + (c + 1) * LANES] = rows.astype(view_ref.dtype)

    hb = _modulated(x_ref[0], g_ref[...], sh_ref[0], sc_ref[0]).astype(BF16)
    q = jnp.dot(hb, wq_ref[...], preferred_element_type=F32) * QK_SCALE
    q_ref[0] = q.astype(q_ref.dtype)
    emit_view(q[:, A_BW:2 * A_BW], qv1_ref, A_BRANCHES[1][1])
    emit_view(q[:, 2 * A_BW:], qv2_ref, A_BRANCHES[2][1])
    kv0_ref[0] = jnp.dot(hb, wkv0_ref[...], preferred_element_type=F32)
    emit_view(jnp.dot(hb, wkv1_ref[...], preferred_element_type=F32), kvv1_ref, A_BRANCHES[1][1])
    kv2 = jnp.dot(hb, wkv2_ref[...], preferred_element_type=F32)
    kv2_ref[0] = kv2
    emit_view(kv2, kvv2_ref, A_BRANCHES[2][1])
    b_ref[0] = jnp.dot(hb, wb_ref[...], preferred_element_type=F32)
    ba = jnp.dot(hb, wba_ref[...], preferred_element_type=F32)
    z = jnp.dot(ba.astype(BF16), wa2_ref[...], preferred_element_type=F32) + ba2_ref[...]
    la_ref[0] = _log_sigmoid(z) * (1.0 / B_GATE_TEMP)


def _ab_in(x, g, shift, scale, w, tm):
    ng, t, d = x.shape
    d1, d2 = A_BRANCHES[1][1], A_BRANCHES[2][1]
    assert A_BRANCHES[0][1] == 1 and tm % (8 * d2) == 0
    sds = jax.ShapeDtypeStruct
    tok = lambda n: pl.BlockSpec((1, tm, n), lambda gi, i: (gi, i, 0))
    view = lambda n, dil: pl.BlockSpec((1, tm // dil, dil * n), lambda gi, i: (gi, i, 0))
    weights = [w["q"], w["kv0"], w["kv1"], w["kv2"], w["b"], w["ba"], w["a2"], w["ba2"]]
    nb_w = 2 * B_QK + 2 * B_OUT
    return pl.pallas_call(
        functools.partial(_ab_in_kernel, tm=tm),
        grid=(ng, t // tm),
        in_specs=[tok(d), pl.BlockSpec((1, d), lambda gi, i: (0, 0)), _mod_spec(shift, tm), _mod_spec(scale, tm)]
                 + [_resident(a.shape) for a in weights],
        out_specs=[tok(3 * A_BW), view(A_BW, d1), view(A_BW, d2), tok(2 * A_BW), view(2 * A_BW, d1),
                   tok(2 * A_BW), view(2 * A_BW, d2), tok(nb_w), tok(B_QK)],
        out_shape=[sds((ng, t, 3 * A_BW), BF16), sds((ng, t // d1, d1 * A_BW), BF16), sds((ng, t // d2, d2 * A_BW), BF16),
                   sds((ng, t, 2 * A_BW), F32), sds((ng, t // d1, d1 * 2 * A_BW), F32),
                   sds((ng, t, 2 * A_BW), F32), sds((ng, t // d2, d2 * 2 * A_BW), F32),
                   sds((ng, t, nb_w), F32), sds((ng, t, B_QK), F32)],
        scratch_shapes=[pltpu.VMEM((2 * A_BW // LANES, tm, LANES), F32)],
        compiler_params=_params(("arbitrary", "arbitrary"), 58),
        name="ab_in_proj",
    )(x, g, shift, scale, *weights)


def _dil_prompt_kernel(q_ref, kc_ref, vc_ref, kp_ref, vp_ref, o_ref, lse_ref, *, span):
    n = pl.program_id(2)
    q = q_ref[0]
    kc = kc_ref[0].astype(BF16)
    vc = vc_ref[0].astype(BF16)
    kp = kp_ref[0].astype(BF16)
    vp = vp_ref[0].astype(BF16)
    qi = _iota((span, span), 0)
    kj = _iota((span, span), 1)
    cur_ok = kj <= qi
    prev_ok = kj >= qi
    prev_off = jnp.where(n > 0, 0.0, NEG)
    heads = [slice(h * HEAD_DIM, (h + 1) * HEAD_DIM) for h in range(A_HEADS)]
    scores = [(lax.dot_general(q[:, sl], kc[:, sl], NT_DIMS, preferred_element_type=F32),
               lax.dot_general(q[:, sl], kp[:, sl], NT_DIMS, preferred_element_type=F32)) for sl in heads]
    probs, lses = [], []
    for sc, sp in scores:
        sc = jnp.where(cur_ok, sc, NEG)
        sp = jnp.where(prev_ok, sp + prev_off, NEG)
        m = jnp.maximum(jnp.max(sc, axis=-1, keepdims=True), jnp.max(sp, axis=-1, keepdims=True))
        pc = jnp.exp(sc - m)
        pp = jnp.exp(sp - m)
        den = jnp.sum(pc, axis=-1, keepdims=True) + jnp.sum(pp, axis=-1, keepdims=True)
        inv = 1.0 / den
        probs.append(((pc * inv).astype(BF16), (pp * inv).astype(BF16)))
        lses.append(jnp.broadcast_to(m + jnp.log(den), (span, HEAD_DIM)))
    outs = [jnp.dot(pc, vc[:, sl], preferred_element_type=F32) + jnp.dot(pp, vp[:, sl], preferred_element_type=F32)
            for (pc, pp), sl in zip(probs, heads)]
    o_ref[0] = jnp.concatenate(outs, axis=1)
    lse_ref[0] = jnp.concatenate(lses, axis=1)


def _dil_prompt(q_v, kv_v, branch, q_col):
    win, dil = A_BRANCHES[branch]
    nbatch, rows, _ = kv_v.shape
    span = win // dil
    assert kv_v.shape[2] == dil * 2 * A_BW and rows % span == 0
    nq = q_v.shape[2] // (dil * A_BW)
    nblk = rows // span
    blk = (1, span, A_BW)
    out_sds = jax.ShapeDtypeStruct((nbatch, rows, dil * A_BW), F32)
    return pl.pallas_call(
        functools.partial(_dil_prompt_kernel, span=span),
        grid=(nbatch, dil, nblk),
        in_specs=[pl.BlockSpec(blk, lambda b, r, n: (b, n, nq * r + q_col)),
                  pl.BlockSpec(blk, lambda b, r, n: (b, n, 2 * r)),
                  pl.BlockSpec(blk, lambda b, r, n: (b, n, 2 * r + 1)),
                  pl.BlockSpec(blk, lambda b, r, n: (b, jnp.maximum(n - 1, 0), 2 * r)),
                  pl.BlockSpec(blk, lambda b, r, n: (b, jnp.maximum(n - 1, 0), 2 * r + 1))],
        out_specs=[pl.BlockSpec(blk, lambda b, r, n: (b, n, r))] * 2,
        out_shape=[out_sds, out_sds],
        compiler_params=_params(("arbitrary",) * 3, 32),
        name=f"dilated_prompt_{branch}",
    )(q_v, kv_v, kv_v, kv_v, kv_v)


def _dil_sample_kernel(q_ref, new0_ref, new1_ref, new2_ref, buf0_ref, buf1_ref, buf2_ref, o_ref):
    outs, lses = [], []
    for g, (new_ref, buf_ref) in enumerate(((new0_ref, buf0_ref), (new1_ref, buf1_ref), (new2_ref, buf2_ref))):
        win, dil = A_BRANCHES[g]
        lw = buf_ref.shape[-1]
        npad = new_ref.shape[-1]
        q = q_ref[0, g]
        t_new = q.shape[1]
        s1 = _bdot_nn(q, buf_ref[0, 0].astype(BF16))
        s2 = _bdot_nn(q, new_ref[0, 0].astype(BF16))
        delta1 = lw + _iota(s1.shape, 1) - _iota(s1.shape, 2)
        delta2 = _iota(s2.shape, 1) - _iota(s2.shape, 2)
        s1 = jnp.where((_mod(delta1, dil) == 0) & (delta1 <= win), s1, NEG)
        s2 = jnp.where((_mod(delta2, dil) == 0) & (delta2 >= 0), s2, NEG)
        m = jnp.maximum(jnp.max(s1, axis=-1, keepdims=True), jnp.max(s2, axis=-1, keepdims=True))
        p1 = jnp.exp(s1 - m)
        p2 = jnp.exp(s2 - m)
        den = jnp.sum(p1, axis=-1, keepdims=True) + jnp.sum(p2, axis=-1, keepdims=True)
        inv = 1.0 / den
        o = _bdot_nt((p1 * inv).astype(BF16), buf_ref[0, 1].astype(BF16))
        o = o + _bdot_nt((p2 * inv).astype(BF16), new_ref[0, 1].astype(BF16))
        outs.append(o)
        lses.append(m + jnp.log(den))
    mx = jnp.maximum(jnp.maximum(lses[0], lses[1]), lses[2])
    es = [jnp.exp(l - mx) for l in lses]
    inv = 1.0 / (es[0] + es[1] + es[2])
    o_ref[0] = (es[0] * outs[0] + es[1] * outs[1] + es[2] * outs[2]) * inv


def _dil_sample(q, new_t, bufs_t):
    db, _, nh, t_new, dh = q.shape
    for (win, _), buf in zip(A_BRANCHES, bufs_t):
        assert buf.shape[-1] == win
    per_b = lambda a: pl.BlockSpec((1,) + a.shape[1:], lambda b: (b,) + (0,) * (a.ndim - 1))
    return pl.pallas_call(
        _dil_sample_kernel,
        grid=(db,),
        in_specs=[per_b(q)] + [per_b(a) for a in new_t] + [per_b(a) for a in bufs_t],
        out_specs=pl.BlockSpec((1, nh, t_new, dh), lambda b: (b, 0, 0, 0)),
        out_shape=jax.ShapeDtypeStruct((db, nh, t_new, dh), F32),
        compiler_params=_params(("arbitrary",), 48),
        name="dilated_sample",
    )(q, *new_t, *bufs_t)


def _gla_kernel(qk_ref, v_ref, r_ref, la_ref, s0_ref, gbo_ref, o_ref, st_ref, *, chunk, sub):
    @pl.when(pl.program_id(1) == 0)
    def _():
        st_ref[0] = s0_ref[0]

    q = qk_ref[0, :, :B_QK] * (B_DK ** -0.5)
    k = qk_ref[0, :, B_QK:]
    v = v_ref[0]
    la = la_ref[0]
    rows_in = chunk
    if chunk < sub:
        grow = lambda a: jnp.concatenate([a, jnp.zeros((sub - chunk, a.shape[1]), F32)], axis=0)
        q, k, v, la = grow(q), grow(k), grow(v), grow(la)
        chunk = sub
    tri = (_iota((chunk, chunk), 0) >= _iota((chunk, chunk), 1)).astype(F32)
    cum = jnp.dot(tri, la, precision=HIGHEST, preferred_element_type=F32)
    last = cum[chunk - 1:chunk, :]
    st = st_ref[0]
    qe = (q * jnp.exp(cum)).astype(BF16)
    kd = (k * jnp.exp(last - cum)).astype(BF16)
    vb = v.astype(BF16)
    heads = [(slice(h * B_DK, (h + 1) * B_DK), slice(h * B_DV, (h + 1) * B_DV)) for h in range(B_HEADS)]
    st_b = st.astype(BF16)
    inter = [lax.dot_general(qe[:, ks], st_b[:, ks], NT_DIMS, preferred_element_type=F32) for ks, _ in heads]
    upd = [lax.dot_general(vb[:, vs], kd[:, ks], TN_DIMS, preferred_element_type=F32) for ks, vs in heads]
    starts = list(range(sub, chunk, sub))
    factored = []
    for r0 in starts:
        ref = cum[r0:r0 + 1, :]
        factored.append(((q[r0:r0 + sub] * jnp.exp(cum[r0:r0 + sub] - ref)).astype(BF16),
                         (k[:r0] * jnp.exp(ref - cum[:r0])).astype(BF16)))
    atts = [[lax.dot_general(qt[:, ks], kt[:, ks], NT_DIMS, preferred_element_type=F32).astype(BF16)
             for qt, kt in factored] for ks, _ in heads]
    o_heads = []
    for (_, vs), inter_h, att_h in zip(heads, inter, atts):
        blocks = [jnp.zeros((sub, B_DV), F32)]
        blocks += [jnp.dot(a, vb[:r0, vs], preferred_element_type=F32) for a, r0 in zip(att_h, starts)]
        o_heads.append(inter_h + jnp.concatenate(blocks, axis=0) if starts else inter_h)
    st_ref[0] = st * jnp.exp(last) + jnp.concatenate(upd, axis=1)
    o = jnp.concatenate(o_heads, axis=1)

    spread = (_div(_iota((B_QK, B_OUT), 0), B_DK) == _div(_iota((B_QK, B_OUT), 1), B_DV)).astype(BF16)
    row_in_block = _mod(_iota((chunk, B_QK), 0), sub)
    for off in range(sub):
        k_s = k if off == 0 else pltpu.roll(k, off, 0)
        c_s = cum if off == 0 else pltpu.roll(cum, off, 0)
        v_s = v if off == 0 else pltpu.roll(v, off, 0)
        term = q * k_s * jnp.exp(jnp.minimum(cum - c_s, 0.0))
        term = jnp.where(row_in_block >= off, term, 0.0)
        att = jnp.dot(term.astype(BF16), spread, preferred_element_type=F32)
        o = o + att * v_s

    o = o[:rows_in]
    r = r_ref[0]
    gate = r * (1.0 / (1.0 + jnp.exp(-r)))
    normed = []
    for h in range(B_HEADS):
        vs = slice(h * B_DV, (h + 1) * B_DV)
        o_h = o[:, vs]
        var = jnp.mean(o_h * o_h, axis=-1, keepdims=True)
        normed.append(o_h * lax.rsqrt(var + RMS_EPS))
    o_ref[0] = (jnp.concatenate(normed, axis=1) * gbo_ref[...] * gate).astype(o_ref.dtype)


def _gla(bproj, log_a, s0_t, g_bo, chunk, sub):
    nb, t, _ = bproj.shape
    assert t % chunk == 0 and (chunk % sub == 0 or (chunk < sub and t == chunk))
    half = 2 * B_QK
    assert half == B_OUT
    col = lambda j: pl.BlockSpec((1, chunk, half), lambda b, c: (b, c, j))
    st_spec = pl.BlockSpec((1, B_DV, B_QK), lambda b, c: (b, 0, 0))
    return pl.pallas_call(
        functools.partial(_gla_kernel, chunk=chunk, sub=sub),
        grid=(nb, t // chunk),
        in_specs=[col(0), col(1), col(2),
                  pl.BlockSpec((1, chunk, B_QK), lambda b, c: (b, c, 0)),
                  st_spec,
                  pl.BlockSpec((1, B_OUT), lambda b, c: (0, 0))],
        out_specs=[pl.BlockSpec((1, chunk, B_OUT), lambda b, c: (b, c, 0)), st_spec],
        out_shape=[jax.ShapeDtypeStruct((nb, t, B_OUT), BF16), jax.ShapeDtypeStruct((nb, B_DV, B_QK), F32)],
        compiler_params=_params(("arbitrary", "arbitrary"), 32),
        name=f"gla_chunk{chunk}",
    )(bproj, bproj, bproj, log_a, s0_t, g_bo)


def _proj_res_kernel(*refs, n_in, mix3):
    if mix3:
        nat_ref, refs = refs[-1], refs[:-1]
        tm = nat_ref.shape[1]
        ncol = A_BW // LANES
        slot = 0
        nat = []
        for ref in refs[:6]:
            dil = ref.shape[2] // A_BW
            if dil == 1:
                nat.append(ref[0])
                continue
            for r in range(dil):
                for c in range(ncol):
                    lanes = slice(r * A_BW + c * LANES, r * A_BW + (c + 1) * LANES)
                    nat_ref[slot * ncol + c, pl.ds(r, tm // dil, stride=dil), :] = ref[0, :, lanes]
            nat.append(jnp.concatenate([nat_ref[slot * ncol + c] for c in range(ncol)], axis=1))
            slot += 1
        o0, o1, o2, l0, l1, l2 = nat
        refs = refs[6:]
        mx = jnp.maximum(jnp.maximum(l0, l1), l2)
        e0, e1, e2 = jnp.exp(l0 - mx), jnp.exp(l1 - mx), jnp.exp(l2 - mx)
        first = ((e0 * o0 + e1 * o1 + e2 * o2) * (1.0 / (e0 + e1 + e2))).astype(BF16)
        ins = [first] + [r[0] for r in refs[:n_in - 1]]
        refs = refs[n_in - 1:]
    else:
        ins = [r[0] for r in refs[:n_in]]
        refs = refs[n_in:]
    w_refs, (x_ref, gate_ref, out_ref) = refs[:n_in], refs[n_in:]
    y = jnp.dot(ins[0], w_refs[0][...], preferred_element_type=F32)
    for a, w_ref in zip(ins[1:], w_refs[1:]):
        y = y + jnp.dot(a, w_ref[...], preferred_element_type=F32)
    out_ref[0] = x_ref[0] + gate_ref[0] * y


def _proj_res(ins, ws, x, gate, tm, mix=None):
    ng, t, d = x.shape
    tok = lambda a: pl.BlockSpec((1, tm, a.shape[-1]), lambda gi, i: (gi, i, 0))
    view = lambda a: pl.BlockSpec((1, tm * a.shape[1] // t, a.shape[2]), lambda gi, i: (gi, i, 0))
    lead = list(mix) if mix is not None else []
    n_in = len(ws)
    n_views = sum(a.shape[1] != t for a in lead)
    return pl.pallas_call(
        functools.partial(_proj_res_kernel, n_in=n_in, mix3=mix is not None),
        grid=(ng, t // tm),
        in_specs=[view(a) for a in lead] + [tok(a) for a in ins] + [_resident(w.shape) for w in ws]
                 + [tok(x), _mod_spec(gate, tm)],
        out_specs=tok(x),
        out_shape=jax.ShapeDtypeStruct(x.shape, F32),
        scratch_shapes=[pltpu.VMEM((n_views * A_BW // LANES, tm, LANES), F32)] if mix is not None else [],
        compiler_params=_params(("arbitrary", "arbitrary"), 48),
        name="proj_residual_mix" if mix is not None else "proj_residual",
    )(*lead, *ins, *ws, x, gate)


def _mlp_kernel(x_ref, g_ref, sh_ref, sc_ref, gate_ref, wu_ref, wd_ref, gf_ref, out_ref, *, fchunk, final_norm):
    x = x_ref[0]
    hb = _modulated(x, g_ref[...], sh_ref[0], sc_ref[0]).astype(BF16)
    acc = jnp.zeros(x.shape, F32)
    for f0 in range(0, wu_ref.shape[1], fchunk):
        u = jnp.maximum(jnp.dot(hb, wu_ref[:, f0:f0 + fchunk], preferred_element_type=F32), 0.0)
        acc = acc + jnp.dot((u * u).astype(BF16), wd_ref[f0:f0 + fchunk, :], preferred_element_type=F32)
    y = x + gate_ref[0] * acc
    if final_norm:
        var = jnp.mean(y * y, axis=-1, keepdims=True)
        y = y * lax.rsqrt(var + RMS_EPS) * gf_ref[...]
    out_ref[0] = y


def _mlp(x, g, shift, scale, gate, w_up, w_down, g_final, tm, final_norm):
    ng, t, d = x.shape
    tok = pl.BlockSpec((1, tm, d), lambda gi, i: (gi, i, 0))
    vec = pl.BlockSpec((1, d), lambda gi, i: (0, 0))
    return pl.pallas_call(
        functools.partial(_mlp_kernel, fchunk=1024, final_norm=final_norm),
        grid=(ng, t // tm),
        in_specs=[tok, vec, _mod_spec(shift, tm), _mod_spec(scale, tm), _mod_spec(gate, tm),
                  _resident(w_up.shape), _resident(w_down.shape), vec],
        out_specs=tok,
        out_shape=jax.ShapeDtypeStruct(x.shape, F32),
        compiler_params=_params(("arbitrary", "arbitrary"), 56),
        name="mlp_final" if final_norm else "mlp",
    )(x, g, shift, scale, gate, w_up, w_down, g_final)


def _bf16_parts(c):
    c1 = c.astype(BF16).astype(F32)
    c2 = (c - c1).astype(BF16).astype(F32)
    c3 = (c - c1 - c2).astype(BF16).astype(F32)
    return c1, c2, c3


def _fox_in_kernel(x_ref, g_ref, sh_ref, sc_ref, wq_ref, wkt_ref, wvt_ref, wf_ref, wft_ref, bf_ref, bft_ref,
                   selq_ref, oneq_ref, selk_ref, onek_ref,
                   qa_ref, kt_ref, vt_ref, kta_ref, vta_ref, lft_ref, carry_row_ref, carry_col_ref, *, tm):
    @pl.when(pl.program_id(1) == 0)
    def _():
        carry_row_ref[...] = jnp.zeros_like(carry_row_ref)
        carry_col_ref[...] = jnp.zeros_like(carry_col_ref)

    dh = HEAD_DIM
    hb = _modulated(x_ref[0], g_ref[...], sh_ref[0], sc_ref[0]).astype(BF16)
    q = jnp.dot(hb, wq_ref[...], preferred_element_type=F32) * QK_SCALE
    kt = lax.dot_general(wkt_ref[...], hb, NT_DIMS, preferred_element_type=F32)
    vt = lax.dot_general(wvt_ref[...], hb, NT_DIMS, preferred_element_type=F32)
    kt_ref[0] = kt
    vt_ref[0] = vt
    lf = _log_sigmoid(jnp.dot(hb, wf_ref[...], preferred_element_type=F32) + bf_ref[...])
    lft = _log_sigmoid(lax.dot_general(wft_ref[...], hb, NT_DIMS, preferred_element_type=F32) + bft_ref[...])
    lft_ref[0] = lft
    upto = (_iota((tm, tm), 0) <= _iota((tm, tm), 1)).astype(F32)
    since = (_iota((tm, tm), 0) >= _iota((tm, tm), 1)).astype(F32)
    cumq = jnp.dot(since, lf, precision=HIGHEST, preferred_element_type=F32) + carry_row_ref[...]
    carry_row_ref[...] = cumq[tm - 1:tm, :]
    cumt = jnp.dot(lft, upto, precision=HIGHEST, preferred_element_type=F32) + carry_col_ref[:, :1]
    carry_col_ref[...] = jnp.broadcast_to(cumt[:, tm - 1:tm], carry_col_ref.shape)

    parts_q = jnp.concatenate(_bf16_parts(cumq), axis=1).astype(BF16)
    bias_q = jnp.dot(parts_q, selq_ref[...], preferred_element_type=F32) + oneq_ref[...]
    parts_k = jnp.concatenate(_bf16_parts(cumt), axis=0).astype(BF16)
    bias_k = jnp.dot(selk_ref[...], parts_k, preferred_element_type=F32) + onek_ref[...]
    ones = jnp.ones((dh, tm), F32)
    qa, kta, vta = [], [], []
    for h in range(C_HEADS):
        sl = slice(h * dh, (h + 1) * dh)
        qa += [q[:, sl], bias_q[:, sl]]
        kta += [kt[sl], bias_k[sl]]
        vta += [vt[sl], ones]
    qa_ref[0] = jnp.concatenate(qa, axis=1).astype(BF16)
    kta_ref[0] = jnp.concatenate(kta, axis=0).astype(BF16)
    vta_ref[0] = jnp.concatenate(vta, axis=0).astype(BF16)


def _fox_in(x, g, shift, scale, w, tm):
    ng, t, d = x.shape
    cw = C_HEADS * HEAD_DIM
    tok = lambda n: pl.BlockSpec((1, tm, n), lambda gi, i: (gi, i, 0))
    tok_t = lambda n: pl.BlockSpec((1, n, tm), lambda gi, i: (gi, 0, i))
    weights = [w["q"], w["kt"], w["vt"], w["f"], w["ft"], w["bf"], w["bft"], w["selq"], w["oneq"], w["selk"], w["onek"]]
    sds = jax.ShapeDtypeStruct
    return pl.pallas_call(
        functools.partial(_fox_in_kernel, tm=tm),
        grid=(ng, t // tm),
        in_specs=[tok(d), pl.BlockSpec((1, d), lambda gi, i: (0, 0)), _mod_spec(shift, tm), _mod_spec(scale, tm)]
                 + [_resident(a.shape) for a in weights],
        out_specs=[tok(2 * cw), tok_t(cw), tok_t(cw), tok_t(2 * cw), tok_t(2 * cw), tok_t(C_HEADS)],
        out_shape=[sds((ng, t, 2 * cw), BF16), sds((ng, cw, t), F32), sds((ng, cw, t), F32),
                   sds((ng, 2 * cw, t), BF16), sds((ng, 2 * cw, t), BF16), sds((ng, C_HEADS, t), F32)],
        scratch_shapes=[pltpu.VMEM((1, C_HEADS), F32), pltpu.VMEM((C_HEADS, 128), F32)],
        compiler_params=_params(("arbitrary", "arbitrary"), 56),
        name="fox_in_proj",
    )(x, g, shift, scale, *weights)


FOX_TQ = 512
FOX_HEAD_GROUP = 4


def _fox_prompt_kernel(q_ref, kt_ref, vt_ref, o_ref, m_ref, acc_ref):
    qi = pl.program_id(1)
    ki = pl.program_id(2)
    tq = FOX_TQ
    hw = 2 * HEAD_DIM

    @pl.when(ki == 0)
    def _():
        m_ref[...] = jnp.full(m_ref.shape, NEG, F32)
        acc_ref[...] = jnp.zeros_like(acc_ref)

    def update(diagonal):
        causal = _iota((tq, tq), 0) >= _iota((tq, tq), 1)
        for h0 in range(0, C_HEADS, FOX_HEAD_GROUP):
            group = range(h0, h0 + FOX_HEAD_GROUP)
            lanes = [slice(h * hw, (h + 1) * hw) for h in group]
            scores = [jnp.dot(q_ref[0, :, sl], kt_ref[0, sl, :], preferred_element_type=F32) for sl in lanes]
            probs, corrs = [], []
            for h, s in zip(group, scores):
                if diagonal:
                    s = jnp.where(causal, s, NEG)
                m_prev = m_ref[h]
                m_new = jnp.maximum(m_prev, jnp.max(s, axis=-1, keepdims=True))
                m_ref[h] = m_new
                corrs.append(jnp.exp(m_prev - m_new))
                probs.append(jnp.exp(s - jnp.concatenate([m_new] * (tq // hw), axis=1)).astype(BF16))
            pvs = [lax.dot_general(p, vt_ref[0, sl, :], NT_DIMS, preferred_element_type=F32)
                   for p, sl in zip(probs, lanes)]
            accs = [acc_ref[:, sl] * corr + pv for sl, corr, pv in zip(lanes, corrs, pvs)]
            if diagonal:
                outs = [acc * (1.0 / pltpu.roll(acc, HEAD_DIM, 1)) for acc in accs]
                for j in range(0, FOX_HEAD_GROUP, 2):
                    pair = jnp.concatenate([outs[j][:, :HEAD_DIM], outs[j + 1][:, :HEAD_DIM]], axis=1)
                    o_ref[0, :, (h0 + j) * HEAD_DIM:(h0 + j + 2) * HEAD_DIM] = pair.astype(o_ref.dtype)
            else:
                for sl, acc in zip(lanes, accs):
                    acc_ref[:, sl] = acc

    @pl.when(ki < qi)
    def _():
        update(False)

    @pl.when(ki == qi)
    def _():
        update(True)


def _fox_prompt(q_aug, kt_aug, vt_aug):
    nb, s, w2 = q_aug.shape
    tq = FOX_TQ
    nq = s // tq
    assert 2 * HEAD_DIM == 128 and FOX_HEAD_GROUP % 2 == 0 and C_HEADS % FOX_HEAD_GROUP == 0
    kv_spec = pl.BlockSpec((1, w2, tq), lambda b, i, j: (b, 0, jnp.minimum(i, j)))
    return pl.pallas_call(
        _fox_prompt_kernel,
        grid=(nb, nq, nq),
        in_specs=[pl.BlockSpec((1, tq, w2), lambda b, i, j: (b, i, 0)), kv_spec, kv_spec],
        out_specs=pl.BlockSpec((1, tq, w2 // 2), lambda b, i, j: (b, i, 0)),
        out_shape=jax.ShapeDtypeStruct((nb, s, w2 // 2), BF16),
        scratch_shapes=[pltpu.VMEM((C_HEADS, tq, 2 * HEAD_DIM), F32), pltpu.VMEM((tq, w2), F32)],
        compiler_params=_params(("arbitrary",) * 3, 48),
        name="fox_prompt",
    )(q_aug, kt_aug, vt_aug)


def _rows_as_batch(x):
    return jnp.concatenate([x[h:h + 1][None] for h in range(x.shape[0])], axis=0)


def _fox_sample_kernel(pt_ref, q_ref, kn_ref, vn_ref, lfn_ref, *refs, n_pp, n_steps):
    del pt_ref
    k_refs, v_refs, lf_refs = refs[:n_pp], refs[n_pp:2 * n_pp], refs[2 * n_pp:3 * n_pp]
    o_ref, gq_ref, carry_ref, m_ref, l_ref, acc_ref = refs[3 * n_pp:]
    step = pl.program_id(1)
    q = q_ref[0]
    nh = q.shape[0]
    page = lf_refs[0].shape[-1]
    npad = lfn_ref.shape[-1]

    def new_token_cumsum():
        upto = (_iota((npad, npad), 0) <= _iota((npad, npad), 1)).astype(F32)
        return _rows_as_batch(jnp.dot(lfn_ref[0], upto, precision=HIGHEST, preferred_element_type=F32))

    def absorb(scores, values):
        m_prev = m_ref[...]
        m_new = m_prev
        for s in scores:
            m_new = jnp.maximum(m_new, jnp.max(s, axis=-1, keepdims=True))
        corr = jnp.exp(m_prev - m_new)
        l_new = l_ref[...] * corr
        acc = acc_ref[...] * corr
        for s, v_t in zip(scores, values):
            p = jnp.exp(s - m_new)
            l_new = l_new + jnp.sum(p, axis=-1, keepdims=True)
            acc = acc + _bdot_nt(p.astype(BF16), v_t)
        l_ref[...] = l_new
        acc_ref[...] = acc
        m_ref[...] = m_new

    @pl.when(step == 0)
    def _():
        g = jnp.broadcast_to(new_token_cumsum(), (nh, q.shape[1], npad))
        own_t = _iota(g.shape, 1) == _iota(g.shape, 2)
        gq_ref[...] = jnp.sum(jnp.where(own_t, g, 0.0), axis=-1, keepdims=True)
        carry_ref[...] = jnp.zeros_like(carry_ref)
        m_ref[...] = jnp.full(m_ref.shape, NEG, F32)
        l_ref[...] = jnp.zeros_like(l_ref)
        acc_ref[...] = jnp.zeros_like(acc_ref)

    lf_all = jnp.concatenate([r[0] for r in lf_refs], axis=0)
    later = jnp.where(_iota((page, 2 * page), 1) < page,
                      (_iota((page, 2 * page), 0) > _iota((page, 2 * page), 1)).astype(F32), 1.0)
    sums = jnp.dot(lf_all, later, precision=HIGHEST, preferred_element_type=F32)
    carry = carry_ref[...]
    scores = []
    for j in reversed(range(n_pp)):
        rows = slice(j * nh, (j + 1) * nh)
        bias = sums[rows, :page] + carry
        carry = carry + sums[rows, page:]
        scores.append(_bdot_nn(q, k_refs[j][0].astype(BF16)) + _rows_as_batch(bias) + gq_ref[...])
    carry_ref[...] = carry
    absorb(scores, [v_refs[j][0].astype(BF16) for j in reversed(range(n_pp))])

    @pl.when(step == n_steps - 1)
    def _():
        s = _bdot_nn(q, kn_ref[0].astype(BF16)) + gq_ref[...] - new_token_cumsum()
        causal = _iota(s.shape, 1) >= _iota(s.shape, 2)
        absorb([jnp.where(causal, s, NEG)], [vn_ref[0].astype(BF16)])
        o_ref[0] = acc_ref[...] * (1.0 / l_ref[...])


FOX_PAGES_PER_STEP = 16


def _fox_sample(q, k_new_t, v_new_t, lf_new_t, cache_k_t, cache_v_t, cache_lf_t, page_table, layer_off):
    db, nh, t_new, dh = q.shape
    n_pages = page_table.shape[1]
    page = cache_k_t.shape[-1]
    n_pp = FOX_PAGES_PER_STEP
    assert n_pages % n_pp == 0
    n_steps = n_pages // n_pp
    per_b = lambda a: pl.BlockSpec((1,) + a.shape[1:], lambda b, i, pt: (b,) + (0,) * (a.ndim - 1))

    def paged(a, j):
        return pl.BlockSpec((1,) + a.shape[1:],
                            lambda b, i, pt: (layer_off + pt[b, n_pages - n_pp * (i + 1) + j],) + (0,) * (a.ndim - 1))

    caches = (cache_k_t, cache_v_t, cache_lf_t)
    return pl.pallas_call(
        functools.partial(_fox_sample_kernel, n_pp=n_pp, n_steps=n_steps),
        grid_spec=pltpu.PrefetchScalarGridSpec(
            num_scalar_prefetch=1,
            grid=(db, n_steps),
            in_specs=[per_b(q), per_b(k_new_t), per_b(v_new_t), per_b(lf_new_t)]
                     + [paged(a, j) for a in caches for j in range(n_pp)],
            out_specs=pl.BlockSpec((1, nh, t_new, dh), lambda b, i, pt: (b, 0, 0, 0)),
            scratch_shapes=[pltpu.VMEM((nh, t_new, 1), F32), pltpu.VMEM((nh, page), F32),
                            pltpu.VMEM((nh, t_new, 1), F32), pltpu.VMEM((nh, t_new, 1), F32),
                            pltpu.VMEM((nh, t_new, dh), F32)]),
        out_shape=jax.ShapeDtypeStruct((db, nh, t_new, dh), F32),
        compiler_params=_params(("arbitrary", "arbitrary"), 52),
        name="fox_sample",
    )(page_table, q, k_new_t, v_new_t, lf_new_t, *[a for a in caches for _ in range(n_pp)])


def _split_cols(w, sizes):
    out, a = [], 0
    for s in sizes:
        out.append(w[:, a:a + s])
        a += s
    return out


def _ab_weights(w_in, w_a2, b_a2):
    aw = len(A_BRANCHES) * A_BW
    aq, ak, av, bq, bk, bv, br, ba = _split_cols(w_in, [aw, aw, aw, B_QK, B_QK, B_OUT, B_OUT, B_GATE_RANK])
    w = {"q": aq.astype(BF16), "b": jnp.concatenate([bq, bk, bv, br], axis=1).astype(BF16),
         "ba": ba.astype(BF16), "a2": w_a2.astype(BF16), "ba2": b_a2.reshape(1, -1)}
    for g in range(len(A_BRANCHES)):
        cols = slice(g * A_BW, (g + 1) * A_BW)
        w[f"kv{g}"] = jnp.concatenate([ak[:, cols], av[:, cols]], axis=1).astype(BF16)
    return w


def _fox_weights(w_in, b_f):
    cw = C_HEADS * HEAD_DIM
    q, k, v, f = _split_cols(w_in, [cw, cw, cw, C_HEADS])
    nparts = 3
    head = jnp.arange(C_HEADS)
    selq = jnp.zeros((nparts * C_HEADS, cw), F32)
    selk = jnp.zeros((cw, nparts * C_HEADS), F32)
    oneq = jnp.zeros((1, cw), F32)
    onek = jnp.zeros((cw, 1), F32)
    for j in range(nparts):
        selq = selq.at[j * C_HEADS + head, head * HEAD_DIM + nparts + j].set(1.0)
        selk = selk.at[head * HEAD_DIM + j, j * C_HEADS + head].set(-1.0)
        oneq = oneq.at[0, head * HEAD_DIM + j].set(1.0)
        onek = onek.at[head * HEAD_DIM + nparts + j, 0].set(1.0)
    return {"q": q.astype(BF16), "kt": k.T.astype(BF16), "vt": v.T.astype(BF16),
            "f": f.astype(BF16), "ft": f.T.astype(BF16), "bf": b_f.reshape(1, -1), "bft": b_f.reshape(-1, 1),
            "selq": selq.astype(BF16), "oneq": oneq, "selk": selk.astype(BF16), "onek": onek}


NEW_TOKEN_PAD = 128


def _pad_tokens(a):
    return jnp.pad(a, ((0, 0),) * (a.ndim - 1) + ((0, NEW_TOKEN_PAD - a.shape[-1]),))


def kernel(x_prompt, x_sample, cache_a0_kv, cache_a1_kv, cache_a2_kv, state_gla, cache_c_k, cache_c_v, cache_c_logf,
           page_table, c_prompt, c_sample, w_ada, b_ada, g_mix, g_mlp, w_in_ab, w_gla_a2, b_gla_a2, g_gla_out, w_out_ab,
           w_in_fox, b_fox_f, w_out_fox, w_up, w_down, g_final):
    nb, seq, d = x_prompt.shape
    db, t_new, _ = x_sample.shape
    depth = w_ada.shape[0]
    n_pool, page = cache_c_k.shape[1], cache_c_k.shape[2]
    tm_p, tm_s = 512, db * t_new

    ada = _adaln(jnp.concatenate([c_prompt, c_sample], axis=0), w_ada, b_ada)

    def terms(layer):
        tp = [ada[layer, :nb, i * d:(i + 1) * d][:, None, :] for i in range(6)]
        ts = [jnp.repeat(ada[layer, nb:, i * d:(i + 1) * d], t_new, axis=0)[None] for i in range(6)]
        return tp, ts

    xp = x_prompt
    xs = x_sample.reshape(1, db * t_new, d)
    a_caches = (cache_a0_kv, cache_a1_kv, cache_a2_kv)
    a_p, a_s = [[], [], []], [[], [], []]
    gla_p, gla_s = [], []
    ck_p, cv_p, cf_p, ck_s, cv_s, cf_s = [], [], [], [], [], []
    g_final2 = g_final.reshape(1, d)

    for layer in range(depth):
        tp, ts = terms(layer)
        g1 = g_mix[layer].reshape(1, d)
        g2 = g_mlp[layer].reshape(1, d)
        last = layer == depth - 1
        if layer % 2 == 0:
            e = layer // 2
            w = _ab_weights(w_in_ab[e], w_gla_a2[e], b_gla_a2[e])
            w_out_a = w_out_ab[e, :A_BW].astype(BF16)
            w_out_b = w_out_ab[e, A_BW:].astype(BF16)
            g_bo = g_gla_out[e].reshape(1, B_OUT)

            q, qv1, qv2, kv0, kvv1, kv2, kvv2, bproj, log_a = _ab_in(xp, g1, tp[0], tp[1], w, tm_p)
            branch = [_dil_prompt(q, kv0, 0, q_col=0), _dil_prompt(qv1, kvv1, 1, q_col=0),
                      _dil_prompt(qv2, kvv2, 2, q_col=0)]
            o_b, st = _gla(bproj, log_a, jnp.zeros((nb, B_DV, B_QK), F32), g_bo, chunk=128, sub=8)
            mix = [o for o, _ in branch] + [l for _, l in branch]
            xp = _proj_res([o_b], [w_out_a, w_out_b], xp, tp[2], tm_p, mix=mix)
            kvs = (kv0, kvv1.reshape(nb, seq, 2 * A_BW), kv2)
            for g, (win, _) in enumerate(A_BRANCHES):
                keep = min(win, seq)
                src = kvv1[:, (seq - keep) // A_BRANCHES[1][1]:] if g == 1 else kvs[g][:, seq - keep:]
                a_p[g].append(src.reshape(nb, keep, 2, A_HEADS, HEAD_DIM))
            gla_p.append(st.reshape(nb, B_DV, B_HEADS, B_DK).transpose(0, 2, 3, 1))

            q, _, _, kv0, kvv1, kv2, _, bproj, log_a = _ab_in(xs, g1, ts[0], ts[1], w, tm_s)
            kv1 = kvv1.reshape(kv0.shape)
            q_s = q.reshape(db, t_new, len(A_BRANCHES), A_HEADS, HEAD_DIM).transpose(0, 2, 3, 1, 4)
            new_t = [a.reshape(db, t_new, 2, A_HEADS, HEAD_DIM).transpose(0, 2, 3, 4, 1) for a in (kv0, kv1, kv2)]
            bufs_t = [c[e].transpose(0, 2, 3, 4, 1) for c in a_caches]
            o_a = _dil_sample(q_s, [_pad_tokens(a) for a in new_t], bufs_t)
            o_a = o_a.transpose(0, 2, 1, 3).reshape(1, db * t_new, A_BW).astype(BF16)
            s0_t = state_gla[e].transpose(0, 3, 1, 2).reshape(db, B_DV, B_QK)
            o_b, st = _gla(bproj.reshape(db, t_new, -1), log_a.reshape(db, t_new, B_QK), s0_t, g_bo,
                           chunk=t_new, sub=16)
            xs = _proj_res([o_a, o_b.reshape(1, db * t_new, B_OUT)], [w_out_a, w_out_b], xs, ts[2], tm_s)
            for g, (win, _) in enumerate(A_BRANCHES):
                full = jnp.concatenate([bufs_t[g], new_t[g]], axis=-1)
                keep = min(win, full.shape[-1])
                a_s[g].append(full[..., full.shape[-1] - keep:].transpose(0, 4, 1, 2, 3))
            gla_s.append(st.reshape(db, B_DV, B_HEADS, B_DK).transpose(0, 2, 3, 1))
        else:
            o_idx = layer // 2
            w = _fox_weights(w_in_fox[o_idx], b_fox_f[o_idx])
            w_out = w_out_fox[o_idx].astype(BF16)
            cw = C_HEADS * HEAD_DIM

            q_aug, kt, vt, kt_aug, vt_aug, lft = _fox_in(xp, g1, tp[0], tp[1], w, tm_p)
            o = _fox_prompt(q_aug, kt_aug, vt_aug)
            xp = _proj_res([o], [w_out], xp, tp[2], tm_p)
            ck_p.append(kt.reshape(nb, C_HEADS, HEAD_DIM, seq).transpose(0, 3, 1, 2))
            cv_p.append(vt.reshape(nb, C_HEADS, HEAD_DIM, seq).transpose(0, 3, 1, 2))
            cf_p.append(lft.transpose(0, 2, 1))

            q_aug, kt, vt, _, _, lft = _fox_in(xs, g1, ts[0], ts[1], w, tm_s)
            q_s = q_aug.reshape(db, t_new, C_HEADS, 2 * HEAD_DIM)[..., :HEAD_DIM].transpose(0, 2, 1, 3)
            k_new_t = kt.reshape(C_HEADS, HEAD_DIM, db, t_new).transpose(2, 0, 1, 3)
            v_new_t = vt.reshape(C_HEADS, HEAD_DIM, db, t_new).transpose(2, 0, 1, 3)
            lf_new_t = lft.reshape(C_HEADS, db, t_new).transpose(1, 0, 2)
            o = _fox_sample(q_s, _pad_tokens(k_new_t), _pad_tokens(v_new_t), _pad_tokens(lf_new_t),
                            cache_c_k.transpose(0, 1, 3, 4, 2).reshape(-1, C_HEADS, HEAD_DIM, page),
                            cache_c_v.transpose(0, 1, 3, 4, 2).reshape(-1, C_HEADS, HEAD_DIM, page),
                            cache_c_logf.transpose(0, 1, 3, 2).reshape(-1, C_HEADS, page), page_table, o_idx * n_pool)
            o = o.transpose(0, 2, 1, 3).reshape(1, db * t_new, cw).astype(BF16)
            xs = _proj_res([o], [w_out], xs, ts[2], tm_s)
            ck_s.append(k_new_t.transpose(0, 3, 1, 2))
            cv_s.append(v_new_t.transpose(0, 3, 1, 2))
            cf_s.append(lf_new_t.transpose(0, 2, 1))

        w_up_b = w_up[layer].astype(BF16)
        w_down_b = w_down[layer].astype(BF16)
        xp = _mlp(xp, g2, tp[3], tp[4], tp[5], w_up_b, w_down_b, g_final2, tm_p, last)
        xs = _mlp(xs, g2, ts[3], ts[4], ts[5], w_up_b, w_down_b, g_final2, tm_s, last)

    stack = lambda parts: jnp.stack(parts, axis=0)
    return (xp, xs.reshape(db, t_new, d),
            stack(a_p[0]), stack(a_p[1]), stack(a_p[2]), stack(gla_p),
            stack(ck_p), stack(cv_p), stack(cf_p),
            stack(a_s[0]), stack(a_s[1]), stack(a_s[2]), stack(gla_s),
            stack(ck_s), stack(cv_s), stack(cf_s))
```

```python
import functools

import jax
import jax.numpy as jnp
from jax import lax
from jax.experimental import pallas as pl
from jax.experimental.pallas import tpu as pltpu

F32 = jnp.float32
BF16 = jnp.bfloat16
HIGHEST = lax.Precision.HIGHEST

HEAD_DIM = 64
RMS_EPS = 1e-6
A_BRANCHES = ((128, 1), (512, 4), (2048, 16))
A_HEADS = 8
A_BW = A_HEADS * HEAD_DIM
B_HEADS = 4
B_DK = 64
B_DV = 128
B_QK = B_HEADS * B_DK
B_OUT = B_HEADS * B_DV
B_GATE_RANK = 16
B_GATE_TEMP = 16.0
C_HEADS = 16
QK_SCALE = HEAD_DIM ** -0.5
NEG = -1e30

V7X_VMEM_BYTES = 64 * 1024 * 1024
LANES = 128
NT_DIMS = (((1,), (1,)), ((), ()))
TN_DIMS = (((0,), (0,)), ((), ()))


def _params(sem, vmem_mb):
    assert vmem_mb * 1024 * 1024 < V7X_VMEM_BYTES
    return pltpu.CompilerParams(dimension_semantics=sem, vmem_limit_bytes=vmem_mb * 1024 * 1024)


def _iota(shape, dim):
    return lax.broadcasted_iota(jnp.int32, shape, dim)


def _div(x, n):
    assert n & (n - 1) == 0
    return x >> (n.bit_length() - 1)


def _mod(x, n):
    assert n & (n - 1) == 0
    return x & (n - 1)


def _bdot_nn(a, b):
    return lax.dot_general(a, b, (((2,), (1,)), ((0,), (0,))), preferred_element_type=F32)


def _bdot_nt(a, b):
    return lax.dot_general(a, b, (((2,), (2,)), ((0,), (0,))), preferred_element_type=F32)


def _resident(shape):
    nd = len(shape)
    return pl.BlockSpec(shape, lambda *_: (0,) * nd, pipeline_mode=pl.Buffered(1))


def _log_sigmoid(z):
    return jnp.minimum(z, 0.0) - jnp.log1p(jnp.exp(-jnp.abs(z)))


def _modulated(x, g, shift, scale):
    var = jnp.mean(x * x, axis=-1, keepdims=True)
    h = x * lax.rsqrt(var + RMS_EPS) * g
    return h * (1.0 + scale) + shift


def _mod_spec(mod, tm):
    d = mod.shape[-1]
    if mod.shape[1] == 1:
        return pl.BlockSpec((1, 1, d), lambda g, i: (g, 0, 0))
    return pl.BlockSpec((1, tm, d), lambda g, i: (g, i, 0))


def _ada_kernel(c_ref, w_ref, b_ref, o_ref):
    o_ref[0] = jnp.dot(c_ref[...].astype(BF16), w_ref[0].astype(BF16),
                       preferred_element_type=F32) + b_ref[0]


def _adaln(c_all, w_ada, b_ada, tn=1024):
    nl, d, n = w_ada.shape
    nb = c_all.shape[0]
    return pl.pallas_call(
        _ada_kernel,
        grid=(nl, n // tn),
        in_specs=[pl.BlockSpec((nb, d), lambda l, j: (0, 0)),
                  pl.BlockSpec((1, d, tn), lambda l, j: (l, 0, j)),
                  pl.BlockSpec((1, 1, tn), lambda l, j: (l, 0, j))],
        out_specs=pl.BlockSpec((1, nb, tn), lambda l, j: (l, 0, j)),
        out_shape=jax.ShapeDtypeStruct((nl, nb, n), F32),
        compiler_params=_params(("arbitrary", "arbitrary"), 32),
        name="adaln",
    )(c_all, w_ada, b_ada.reshape(nl, 1, n))


def _ab_in_kernel(x_ref, g_ref, sh_ref, sc_ref, wq_ref, wkv0_ref, wkv1_ref, wkv2_ref, wb_ref, wba_ref,
                  wa2_ref, ba2_ref, q_ref, qv1_ref, qv2_ref, kv0_ref, kvv1_ref, kv2_ref, kvv2_ref, b_ref, la_ref,
                  stage_ref, *, tm):
    def emit_view(val, view_ref, dil):
        width = val.shape[1]
        for c in range(width // LANES):
            stage_ref[c] = val[:, c * LANES:(c + 1) * LANES]
        for r in range(dil):
            for c in range(width // LANES):
                rows = stage_ref[c, pl.ds(r, tm // dil, stride=dil), :]
                view_ref[0, :, r * width + c * LANES:r * width + (c + 1) * LANES] = rows.astype(view_ref.dtype)

    hb = _modulated(x_ref[0], g_ref[...], sh_ref[0], sc_ref[0]).astype(BF16)
    q = jnp.dot(hb, wq_ref[...], preferred_element_type=F32) * QK_SCALE
    q_ref[0] = q.astype(q_ref.dtype)
    emit_view(q[:, A_BW:2 * A_BW], qv1_ref, A_BRANCHES[1][1])
    emit_view(q[:, 2 * A_BW:], qv2_ref, A_BRANCHES[2][1])
    kv0_ref[0] = jnp.dot(hb, wkv0_ref[...], preferred_element_type=F32)
    emit_view(jnp.dot(hb, wkv1_ref[...], preferred_element_type=F32), kvv1_ref, A_BRANCHES[1][1])
    kv2 = jnp.dot(hb, wkv2_ref[...], preferred_element_type=F32)
    kv2_ref[0] = kv2
    emit_view(kv2, kvv2_ref, A_BRANCHES[2][1])
    b_ref[0] = jnp.dot(hb, wb_ref[...], preferred_element_type=F32)
    ba = jnp.dot(hb, wba_ref[...], preferred_element_type=F32)
    z = jnp.dot(ba.astype(BF16), wa2_ref[...], preferred_element_type=F32) + ba2_ref[...]
    la_ref[0] = _log_sigmoid(z) * (1.0 / B_GATE_TEMP)


def _ab_in(x, g, shift, scale, w, tm):
    ng, t, d = x.shape
    d1, d2 = A_BRANCHES[1][1], A_BRANCHES[2][1]
    assert A_BRANCHES[0][1] == 1 and tm % (8 * d2) == 0
    sds = jax.ShapeDtypeStruct
    tok = lambda n: pl.BlockSpec((1, tm, n), lambda gi, i: (gi, i, 0))
    view = lambda n, dil: pl.BlockSpec((1, tm // dil, dil * n), lambda gi, i: (gi, i, 0))
    weights = [w["q"], w["kv0"], w["kv1"], w["kv2"], w["b"], w["ba"], w["a2"], w["ba2"]]
    nb_w = 2 * B_QK + 2 * B_OUT
    return pl.pallas_call(
        functools.partial(_ab_in_kernel, tm=tm),
        grid=(ng, t // tm),
        in_specs=[tok(d), pl.BlockSpec((1, d), lambda gi, i: (0, 0)), _mod_spec(shift, tm), _mod_spec(scale, tm)]
                 + [_resident(a.shape) for a in weights],
        out_specs=[tok(3 * A_BW), view(A_BW, d1), view(A_BW, d2), tok(2 * A_BW), view(2 * A_BW, d1),
                   tok(2 * A_BW), view(2 * A_BW, d2), tok(nb_w), tok(B_QK)],
        out_shape=[sds((ng, t, 3 * A_BW), BF16), sds((ng, t // d1, d1 * A_BW), BF16), sds((ng, t // d2, d2 * A_BW), BF16),
                   sds((ng, t, 2 * A_BW), F32), sds((ng, t // d1, d1 * 2 * A_BW), F32),
                   sds((ng, t, 2 * A_BW), F32), sds((ng, t // d2, d2 * 2 * A_BW), F32),
                   sds((ng, t, nb_w), F32), sds((ng, t, B_QK), F32)],
        scratch_shapes=[pltpu.VMEM((2 * A_BW // LANES, tm, LANES), F32)],
        compiler_params=_params(("arbitrary", "arbitrary"), 58),
        name="ab_in_proj",
    )(x, g, shift, scale, *weights)


DIL_BLOCKS_PER_STEP = 4


def _dil_prompt_kernel(q_ref, kc_ref, vc_ref, kp_ref, vp_ref, o_ref, lse_ref, *, span, nsub):
    n = pl.program_id(2)
    q = q_ref[0]
    kc = kc_ref[0].astype(BF16)
    vc = vc_ref[0].astype(BF16)
    kp = kp_ref[0].astype(BF16)
    vp = vp_ref[0].astype(BF16)
    qi = _iota((span, span), 0)
    kj = _iota((span, span), 1)
    cur_ok = kj <= qi
    prev_ok = kj >= qi
    first_off = jnp.where(n > 0, 0.0, NEG)
    units = []
    for j in range(nsub):
        rows = slice(j * span, (j + 1) * span)
        k_prev, v_prev = (kp, vp) if j == 0 else (kc[(j - 1) * span:j * span], vc[(j - 1) * span:j * span])
        for h in range(A_HEADS):
            units.append((rows, slice(h * HEAD_DIM, (h + 1) * HEAD_DIM), k_prev, v_prev, j == 0))
    scores = [(lax.dot_general(q[rows, sl], kc[rows, sl], NT_DIMS, preferred_element_type=F32),
               lax.dot_general(q[rows, sl], k_prev[:, sl], NT_DIMS, preferred_element_type=F32))
              for rows, sl, k_prev, _, _ in units]
    probs, lses = [], []
    for (sc, sp), (_, _, _, _, is_first) in zip(scores, units):
        sc = jnp.where(cur_ok, sc, NEG)
        sp = jnp.where(prev_ok, sp + first_off if is_first else sp, NEG)
        m = jnp.maximum(jnp.max(sc, axis=-1, keepdims=True), jnp.max(sp, axis=-1, keepdims=True))
        pc = jnp.exp(sc - m)
        pp = jnp.exp(sp - m)
        den = jnp.sum(pc, axis=-1, keepdims=True) + jnp.sum(pp, axis=-1, keepdims=True)
        inv = 1.0 / den
        probs.append(((pc * inv).astype(BF16), (pp * inv).astype(BF16)))
        lses.append(jnp.broadcast_to(m + jnp.log(den), (span, HEAD_DIM)))
    outs = [jnp.dot(pc, vc[rows, sl], preferred_element_type=F32) + jnp.dot(pp, v_prev[:, sl], preferred_element_type=F32)
            for (pc, pp), (rows, sl, _, v_prev, _) in zip(probs, units)]
    for j in range(nsub):
        rows = slice(j * span, (j + 1) * span)
        o_ref[0, rows, :] = jnp.concatenate(outs[j * A_HEADS:(j + 1) * A_HEADS], axis=1)
        lse_ref[0, rows, :] = jnp.concatenate(lses[j * A_HEADS:(j + 1) * A_HEADS], axis=1)


def _dil_prompt(q_v, kv_v, branch, q_col):
    win, dil = A_BRANCHES[branch]
    nbatch, rows, _ = kv_v.shape
    span = win // dil
    assert kv_v.shape[2] == dil * 2 * A_BW and rows % span == 0
    nq = q_v.shape[2] // (dil * A_BW)
    nsub = DIL_BLOCKS_PER_STEP if (rows // span) % DIL_BLOCKS_PER_STEP == 0 else 1
    nblk = rows // (span * nsub)
    blk = (1, nsub * span, A_BW)
    before = (1, span, A_BW)
    out_sds = jax.ShapeDtypeStruct((nbatch, rows, dil * A_BW), F32)
    return pl.pallas_call(
        functools.partial(_dil_prompt_kernel, span=span, nsub=nsub),
        grid=(nbatch, dil, nblk),
        in_specs=[pl.BlockSpec(blk, lambda b, r, n: (b, n, nq * r + q_col)),
                  pl.BlockSpec(blk, lambda b, r, n: (b, n, 2 * r)),
                  pl.BlockSpec(blk, lambda b, r, n: (b, n, 2 * r + 1)),
                  pl.BlockSpec(before, lambda b, r, n: (b, jnp.maximum(nsub * n - 1, 0), 2 * r)),
                  pl.BlockSpec(before, lambda b, r, n: (b, jnp.maximum(nsub * n - 1, 0), 2 * r + 1))],
        out_specs=[pl.BlockSpec(blk, lambda b, r, n: (b, n, r))] * 2,
        out_shape=[out_sds, out_sds],
        compiler_params=_params(("arbitrary",) * 3, 32),
        name=f"dilated_prompt_{branch}",
    )(q_v, kv_v, kv_v, kv_v, kv_v)


def _dil_sample_kernel(q_ref, new0_ref, new1_ref, new2_ref, buf0_ref, buf1_ref, buf2_ref, o_ref):
    outs, lses = [], []
    for g, (new_ref, buf_ref) in enumerate(((new0_ref, buf0_ref), (new1_ref, buf1_ref), (new2_ref, buf2_ref))):
        win, dil = A_BRANCHES[g]
        lw = buf_ref.shape[-1]
        npad = new_ref.shape[-1]
        q = q_ref[0, g]
        t_new = q.shape[1]
        s1 = _bdot_nn(q, buf_ref[0, 0].astype(BF16))
        s2 = _bdot_nn(q, new_ref[0, 0].astype(BF16))
        delta1 = lw + _iota(s1.shape, 1) - _iota(s1.shape, 2)
        delta2 = _iota(s2.shape, 1) - _iota(s2.shape, 2)
        s1 = jnp.where((_mod(delta1, dil) == 0) & (delta1 <= win), s1, NEG)
        s2 = jnp.where((_mod(delta2, dil) == 0) & (delta2 >= 0), s2, NEG)
        m = jnp.maximum(jnp.max(s1, axis=-1, keepdims=True), jnp.max(s2, axis=-1, keepdims=True))
        p1 = jnp.exp(s1 - m)
        p2 = jnp.exp(s2 - m)
        den = jnp.sum(p1, axis=-1, keepdims=True) + jnp.sum(p2, axis=-1, keepdims=True)
        inv = 1.0 / den
        o = _bdot_nt((p1 * inv).astype(BF16), buf_ref[0, 1].astype(BF16))
        o = o + _bdot_nt((p2 * inv).astype(BF16), new_ref[0, 1].astype(BF16))
        outs.append(o)
        lses.append(m + jnp.log(den))
    mx = jnp.maximum(jnp.maximum(lses[0], lses[1]), lses[2])
    es = [jnp.exp(l - mx) for l in lses]
    inv = 1.0 / (es[0] + es[1] + es[2])
    o_ref[0] = (es[0] * outs[0] + es[1] * outs[1] + es[2] * outs[2]) * inv


def _dil_sample(q, new_t, bufs_t):
    db, _, nh, t_new, dh = q.shape
    for (win, _), buf in zip(A_BRANCHES, bufs_t):
        assert buf.shape[-1] == win
    per_b = lambda a: pl.BlockSpec((1,) + a.shape[1:], lambda b: (b,) + (0,) * (a.ndim - 1))
    return pl.pallas_call(
        _dil_sample_kernel,
        grid=(db,),
        in_specs=[per_b(q)] + [per_b(a) for a in new_t] + [per_b(a) for a in bufs_t],
        out_specs=pl.BlockSpec((1, nh, t_new, dh), lambda b: (b, 0, 0, 0)),
        out_shape=jax.ShapeDtypeStruct((db, nh, t_new, dh), F32),
        compiler_params=_params(("arbitrary",), 48),
        name="dilated_sample",
    )(q, *new_t, *bufs_t)


def _gla_kernel(qk_ref, v_ref, r_ref, la_ref, s0_ref, gbo_ref, o_ref, st_ref, *, chunk, sub):
    @pl.when(pl.program_id(1) == 0)
    def _():
        st_ref[0] = s0_ref[0]

    q = qk_ref[0, :, :B_QK] * (B_DK ** -0.5)
    k = qk_ref[0, :, B_QK:]
    v = v_ref[0]
    la = la_ref[0]
    rows_in = chunk
    if chunk < sub:
        grow = lambda a: jnp.concatenate([a, jnp.zeros((sub - chunk, a.shape[1]), F32)], axis=0)
        q, k, v, la = grow(q), grow(k), grow(v), grow(la)
        chunk = sub
    tri = (_iota((chunk, chunk), 0) >= _iota((chunk, chunk), 1)).astype(F32)
    cum = jnp.dot(tri, la, precision=HIGHEST, preferred_element_type=F32)
    last = cum[chunk - 1:chunk, :]
    st = st_ref[0]
    qe = (q * jnp.exp(cum)).astype(BF16)
    kd = (k * jnp.exp(last - cum)).astype(BF16)
    vb = v.astype(BF16)
    heads = [(slice(h * B_DK, (h + 1) * B_DK), slice(h * B_DV, (h + 1) * B_DV)) for h in range(B_HEADS)]
    st_b = st.astype(BF16)
    inter = [lax.dot_general(qe[:, ks], st_b[:, ks], NT_DIMS, preferred_element_type=F32) for ks, _ in heads]
    upd = [lax.dot_general(vb[:, vs], kd[:, ks], TN_DIMS, preferred_element_type=F32) for ks, vs in heads]
    starts = list(range(sub, chunk, sub))
    factored = []
    for r0 in starts:
        ref = cum[r0:r0 + 1, :]
        factored.append(((q[r0:r0 + sub] * jnp.exp(cum[r0:r0 + sub] - ref)).astype(BF16),
                         (k[:r0] * jnp.exp(ref - cum[:r0])).astype(BF16)))
    atts = [[lax.dot_general(qt[:, ks], kt[:, ks], NT_DIMS, preferred_element_type=F32).astype(BF16)
             for qt, kt in factored] for ks, _ in heads]
    o_heads = []
    for (_, vs), inter_h, att_h in zip(heads, inter, atts):
        blocks = [jnp.zeros((sub, B_DV), F32)]
        blocks += [jnp.dot(a, vb[:r0, vs], preferred_element_type=F32) for a, r0 in zip(att_h, starts)]
        o_heads.append(inter_h + jnp.concatenate(blocks, axis=0) if starts else inter_h)
    st_ref[0] = st * jnp.exp(last) + jnp.concatenate(upd, axis=1)
    o = jnp.concatenate(o_heads, axis=1)

    spread = (_div(_iota((B_QK, B_OUT), 0), B_DK) == _div(_iota((B_QK, B_OUT), 1), B_DV)).astype(BF16)
    row_in_block = _mod(_iota((chunk, B_QK), 0), sub)
    for off in range(sub):
        k_s = k if off == 0 else pltpu.roll(k, off, 0)
        c_s = cum if off == 0 else pltpu.roll(cum, off, 0)
        v_s = v if off == 0 else pltpu.roll(v, off, 0)
        term = q * k_s * jnp.exp(jnp.minimum(cum - c_s, 0.0))
        term = jnp.where(row_in_block >= off, term, 0.0)
        att = jnp.dot(term.astype(BF16), spread, preferred_element_type=F32)
        o = o + att * v_s

    o = o[:rows_in]
    r = r_ref[0]
    gate = r * (1.0 / (1.0 + jnp.exp(-r)))
    normed = []
    for h in range(B_HEADS):
        vs = slice(h * B_DV, (h + 1) * B_DV)
        o_h = o[:, vs]
        var = jnp.mean(o_h * o_h, axis=-1, keepdims=True)
        normed.append(o_h * lax.rsqrt(var + RMS_EPS))
    o_ref[0] = (jnp.concatenate(normed, axis=1) * gbo_ref[...] * gate).astype(o_ref.dtype)


def _gla(bproj, log_a, s0_t, g_bo, chunk, sub):
    nb, t, _ = bproj.shape
    assert t % chunk == 0 and (chunk % sub == 0 or (chunk < sub and t == chunk))
    half = 2 * B_QK
    assert half == B_OUT
    col = lambda j: pl.BlockSpec((1, chunk, half), lambda b, c: (b, c, j))
    st_spec = pl.BlockSpec((1, B_DV, B_QK), lambda b, c: (b, 0, 0))
    return pl.pallas_call(
        functools.partial(_gla_kernel, chunk=chunk, sub=sub),
        grid=(nb, t // chunk),
        in_specs=[col(0), col(1), col(2),
                  pl.BlockSpec((1, chunk, B_QK), lambda b, c: (b, c, 0)),
                  st_spec,
                  pl.BlockSpec((1, B_OUT), lambda b, c: (0, 0))],
        out_specs=[pl.BlockSpec((1, chunk, B_OUT), lambda b, c: (b, c, 0)), st_spec],
        out_shape=[jax.ShapeDtypeStruct((nb, t, B_OUT), BF16), jax.ShapeDtypeStruct((nb, B_DV, B_QK), F32)],
        compiler_params=_params(("arbitrary", "arbitrary"), 32),
        name=f"gla_chunk{chunk}",
    )(bproj, bproj, bproj, log_a, s0_t, g_bo)


def _proj_res_kernel(*refs, n_in, mix3):
    if mix3:
        nat_ref, refs = refs[-1], refs[:-1]
        tm = nat_ref.shape[1]
        ncol = A_BW // LANES
        slot = 0
        nat = []
        for ref in refs[:6]:
            dil = ref.shape[2] // A_BW
            if dil == 1:
                nat.append(ref[0])
                continue
            for r in range(dil):
                for c in range(ncol):
                    lanes = slice(r * A_BW + c * LANES, r * A_BW + (c + 1) * LANES)
                    nat_ref[slot * ncol + c, pl.ds(r, tm // dil, stride=dil), :] = ref[0, :, lanes]
            nat.append(jnp.concatenate([nat_ref[slot * ncol + c] for c in range(ncol)], axis=1))
            slot += 1
        o0, o1, o2, l0, l1, l2 = nat
        refs = refs[6:]
        mx = jnp.maximum(jnp.maximum(l0, l1), l2)
        e0, e1, e2 = jnp.exp(l0 - mx), jnp.exp(l1 - mx), jnp.exp(l2 - mx)
        first = ((e0 * o0 + e1 * o1 + e2 * o2) * (1.0 / (e0 + e1 + e2))).astype(BF16)
        ins = [first] + [r[0] for r in refs[:n_in - 1]]
        refs = refs[n_in - 1:]
    else:
        ins = [r[0] for r in refs[:n_in]]
        refs = refs[n_in:]
    w_refs, (x_ref, gate_ref, out_ref) = refs[:n_in], refs[n_in:]
    y = jnp.dot(ins[0], w_refs[0][...], preferred_element_type=F32)
    for a, w_ref in zip(ins[1:], w_refs[1:]):
        y = y + jnp.dot(a, w_ref[...], preferred_element_type=F32)
    out_ref[0] = x_ref[0] + gate_ref[0] * y


def _proj_res(ins, ws, x, gate, tm, mix=None):
    ng, t, d = x.shape
    tok = lambda a: pl.BlockSpec((1, tm, a.shape[-1]), lambda gi, i: (gi, i, 0))
    view = lambda a: pl.BlockSpec((1, tm * a.shape[1] // t, a.shape[2]), lambda gi, i: (gi, i, 0))
    lead = list(mix) if mix is not None else []
    n_in = len(ws)
    n_views = sum(a.shape[1] != t for a in lead)
    return pl.pallas_call(
        functools.partial(_proj_res_kernel, n_in=n_in, mix3=mix is not None),
        grid=(ng, t // tm),
        in_specs=[view(a) for a in lead] + [tok(a) for a in ins] + [_resident(w.shape) for w in ws]
                 + [tok(x), _mod_spec(gate, tm)],
        out_specs=tok(x),
        out_shape=jax.ShapeDtypeStruct(x.shape, F32),
        scratch_shapes=[pltpu.VMEM((n_views * A_BW // LANES, tm, LANES), F32)] if mix is not None else [],
        compiler_params=_params(("arbitrary", "arbitrary"), 48),
        name="proj_residual_mix" if mix is not None else "proj_residual",
    )(*lead, *ins, *ws, x, gate)


def _mlp_kernel(x_ref, g_ref, sh_ref, sc_ref, gate_ref, wu_ref, wd_ref, gf_ref, out_ref, *, fchunk, final_norm):
    x = x_ref[0]
    hb = _modulated(x, g_ref[...], sh_ref[0], sc_ref[0]).astype(BF16)
    acc = jnp.zeros(x.shape, F32)
    for f0 in range(0, wu_ref.shape[1], fchunk):
        u = jnp.maximum(jnp.dot(hb, wu_ref[:, f0:f0 + fchunk], preferred_element_type=F32), 0.0)
        acc = acc + jnp.dot((u * u).astype(BF16), wd_ref[f0:f0 + fchunk, :], preferred_element_type=F32)
    y = x + gate_ref[0] * acc
    if final_norm:
        var = jnp.mean(y * y, axis=-1, keepdims=True)
        y = y * lax.rsqrt(var + RMS_EPS) * gf_ref[...]
    out_ref[0] = y


def _mlp(x, g, shift, scale, gate, w_up, w_down, g_final, tm, final_norm):
    ng, t, d = x.shape
    tok = pl.BlockSpec((1, tm, d), lambda gi, i: (gi, i, 0))
    vec = pl.BlockSpec((1, d), lambda gi, i: (0, 0))
    return pl.pallas_call(
        functools.partial(_mlp_kernel, fchunk=1024, final_norm=final_norm),
        grid=(ng, t // tm),
        in_specs=[tok, vec, _mod_spec(shift, tm), _mod_spec(scale, tm), _mod_spec(gate, tm),
                  _resident(w_up.shape), _resident(w_down.shape), vec],
        out_specs=tok,
        out_shape=jax.ShapeDtypeStruct(x.shape, F32),
        compiler_params=_params(("arbitrary", "arbitrary"), 56),
        name="mlp_final" if final_norm else "mlp",
    )(x, g, shift, scale, gate, w_up, w_down, g_final)


def _bf16_parts(c):
    c1 = c.astype(BF16).astype(F32)
    c2 = (c - c1).astype(BF16).astype(F32)
    c3 = (c - c1 - c2).astype(BF16).astype(F32)
    return c1, c2, c3


def _fox_in_kernel(x_ref, g_ref, sh_ref, sc_ref, wq_ref, wkt_ref, wvt_ref, wf_ref, wft_ref, bf_ref, bft_ref,
                   selq_ref, oneq_ref, selk_ref, onek_ref,
                   qa_ref, kt_ref, vt_ref, kta_ref, vta_ref, lft_ref, carry_row_ref, carry_col_ref, *, tm):
    @pl.when(pl.program_id(1) == 0)
    def _():
        carry_row_ref[...] = jnp.zeros_like(carry_row_ref)
        carry_col_ref[...] = jnp.zeros_like(carry_col_ref)

    dh = HEAD_DIM
    hb = _modulated(x_ref[0], g_ref[...], sh_ref[0], sc_ref[0]).astype(BF16)
    q = jnp.dot(hb, wq_ref[...], preferred_element_type=F32) * QK_SCALE
    kt = lax.dot_general(wkt_ref[...], hb, NT_DIMS, preferred_element_type=F32)
    vt = lax.dot_general(wvt_ref[...], hb, NT_DIMS, preferred_element_type=F32)
    kt_ref[0] = kt
    vt_ref[0] = vt
    lf = _log_sigmoid(jnp.dot(hb, wf_ref[...], preferred_element_type=F32) + bf_ref[...])
    lft = _log_sigmoid(lax.dot_general(wft_ref[...], hb, NT_DIMS, preferred_element_type=F32) + bft_ref[...])
    lft_ref[0] = lft
    upto = (_iota((tm, tm), 0) <= _iota((tm, tm), 1)).astype(F32)
    since = (_iota((tm, tm), 0) >= _iota((tm, tm), 1)).astype(F32)
    cumq = jnp.dot(since, lf, precision=HIGHEST, preferred_element_type=F32) + carry_row_ref[...]
    carry_row_ref[...] = cumq[tm - 1:tm, :]
    cumt = jnp.dot(lft, upto, precision=HIGHEST, preferred_element_type=F32) + carry_col_ref[:, :1]
    carry_col_ref[...] = jnp.broadcast_to(cumt[:, tm - 1:tm], carry_col_ref.shape)

    parts_q = jnp.concatenate(_bf16_parts(cumq), axis=1).astype(BF16)
    bias_q = jnp.dot(parts_q, selq_ref[...], preferred_element_type=F32) + oneq_ref[...]
    parts_k = jnp.concatenate(_bf16_parts(cumt), axis=0).astype(BF16)
    bias_k = jnp.dot(selk_ref[...], parts_k, preferred_element_type=F32) + onek_ref[...]
    ones = jnp.ones((dh, tm), F32)
    qa, kta, vta = [], [], []
    for h in range(C_HEADS):
        sl = slice(h * dh, (h + 1) * dh)
        qa += [q[:, sl], bias_q[:, sl]]
        kta += [kt[sl], bias_k[sl]]
        vta += [vt[sl], ones]
    qa_ref[0] = jnp.concatenate(qa, axis=1).astype(BF16)
    kta_ref[0] = jnp.concatenate(kta, axis=0).astype(BF16)
    vta_ref[0] = jnp.concatenate(vta, axis=0).astype(BF16)


def _fox_in(x, g, shift, scale, w, tm):
    ng, t, d = x.shape
    cw = C_HEADS * HEAD_DIM
    tok = lambda n: pl.BlockSpec((1, tm, n), lambda gi, i: (gi, i, 0))
    tok_t = lambda n: pl.BlockSpec((1, n, tm), lambda gi, i: (gi, 0, i))
    weights = [w["q"], w["kt"], w["vt"], w["f"], w["ft"], w["bf"], w["bft"], w["selq"], w["oneq"], w["selk"], w["onek"]]
    sds = jax.ShapeDtypeStruct
    return pl.pallas_call(
        functools.partial(_fox_in_kernel, tm=tm),
        grid=(ng, t // tm),
        in_specs=[tok(d), pl.BlockSpec((1, d), lambda gi, i: (0, 0)), _mod_spec(shift, tm), _mod_spec(scale, tm)]
                 + [_resident(a.shape) for a in weights],
        out_specs=[tok(2 * cw), tok_t(cw), tok_t(cw), tok_t(2 * cw), tok_t(2 * cw), tok_t(C_HEADS)],
        out_shape=[sds((ng, t, 2 * cw), BF16), sds((ng, cw, t), F32), sds((ng, cw, t), F32),
                   sds((ng, 2 * cw, t), BF16), sds((ng, 2 * cw, t), BF16), sds((ng, C_HEADS, t), F32)],
        scratch_shapes=[pltpu.VMEM((1, C_HEADS), F32), pltpu.VMEM((C_HEADS, 128), F32)],
        compiler_params=_params(("arbitrary", "arbitrary"), 56),
        name="fox_in_proj",
    )(x, g, shift, scale, *weights)


FOX_TQ = 512
FOX_HEAD_GROUP = 4


def _fox_prompt_kernel(q_ref, kt_ref, vt_ref, o_ref, m_ref, acc_ref):
    qi = pl.program_id(1)
    ki = pl.program_id(2)
    tq = FOX_TQ
    hw = 2 * HEAD_DIM

    @pl.when(ki == 0)
    def _():
        m_ref[...] = jnp.full(m_ref.shape, NEG, F32)
        acc_ref[...] = jnp.zeros_like(acc_ref)

    def update(diagonal):
        causal = _iota((tq, tq), 0) >= _iota((tq, tq), 1)
        for h0 in range(0, C_HEADS, FOX_HEAD_GROUP):
            group = range(h0, h0 + FOX_HEAD_GROUP)
            lanes = [slice(h * hw, (h + 1) * hw) for h in group]
            scores = [jnp.dot(q_ref[0, :, sl], kt_ref[0, sl, :], preferred_element_type=F32) for sl in lanes]
            probs, corrs = [], []
            for h, s in zip(group, scores):
                if diagonal:
                    s = jnp.where(causal, s, NEG)
                m_prev = m_ref[h]
                m_new = jnp.maximum(m_prev, jnp.max(s, axis=-1, keepdims=True))
                m_ref[h] = m_new
                corrs.append(jnp.exp(m_prev - m_new))
                probs.append(jnp.exp(s - jnp.concatenate([m_new] * (tq // hw), axis=1)).astype(BF16))
            pvs = [lax.dot_general(p, vt_ref[0, sl, :], NT_DIMS, preferred_element_type=F32)
                   for p, sl in zip(probs, lanes)]
            accs = [acc_ref[:, sl] * corr + pv for sl, corr, pv in zip(lanes, corrs, pvs)]
            if diagonal:
                outs = [acc * (1.0 / pltpu.roll(acc, HEAD_DIM, 1)) for acc in accs]
                for j in range(0, FOX_HEAD_GROUP, 2):
                    pair = jnp.concatenate([outs[j][:, :HEAD_DIM], outs[j + 1][:, :HEAD_DIM]], axis=1)
                    o_ref[0, :, (h0 + j) * HEAD_DIM:(h0 + j + 2) * HEAD_DIM] = pair.astype(o_ref.dtype)
            else:
                for sl, acc in zip(lanes, accs):
                    acc_ref[:, sl] = acc

    @pl.when(ki < qi)
    def _():
        update(False)

    @pl.when(ki == qi)
    def _():
        update(True)


def _fox_prompt(q_aug, kt_aug, vt_aug):
    nb, s, w2 = q_aug.shape
    tq = FOX_TQ
    nq = s // tq
    assert 2 * HEAD_DIM == 128 and FOX_HEAD_GROUP % 2 == 0 and C_HEADS % FOX_HEAD_GROUP == 0
    kv_spec = pl.BlockSpec((1, w2, tq), lambda b, i, j: (b, 0, jnp.minimum(i, j)))
    return pl.pallas_call(
        _fox_prompt_kernel,
        grid=(nb, nq, nq),
        in_specs=[pl.BlockSpec((1, tq, w2), lambda b, i, j: (b, i, 0)), kv_spec, kv_spec],
        out_specs=pl.BlockSpec((1, tq, w2 // 2), lambda b, i, j: (b, i, 0)),
        out_shape=jax.ShapeDtypeStruct((nb, s, w2 // 2), BF16),
        scratch_shapes=[pltpu.VMEM((C_HEADS, tq, 2 * HEAD_DIM), F32), pltpu.VMEM((tq, w2), F32)],
        compiler_params=_params(("arbitrary",) * 3, 48),
        name="fox_prompt",
    )(q_aug, kt_aug, vt_aug)


def _rows_as_batch(x):
    return jnp.concatenate([x[h:h + 1][None] for h in range(x.shape[0])], axis=0)


def _fox_sample_kernel(pt_ref, q_ref, kn_ref, vn_ref, lfn_ref, *refs, n_pp, n_steps):
    del pt_ref
    k_refs, v_refs, lf_refs = refs[:n_pp], refs[n_pp:2 * n_pp], refs[2 * n_pp:3 * n_pp]
    o_ref, gq_ref, carry_ref, m_ref, l_ref, acc_ref = refs[3 * n_pp:]
    step = pl.program_id(1)
    q = q_ref[0]
    nh = q.shape[0]
    page = lf_refs[0].shape[-1]
    npad = lfn_ref.shape[-1]

    def new_token_cumsum():
        upto = (_iota((npad, npad), 0) <= _iota((npad, npad), 1)).astype(F32)
        return _rows_as_batch(jnp.dot(lfn_ref[0], upto, precision=HIGHEST, preferred_element_type=F32))

    def absorb(scores, values):
        m_prev = m_ref[...]
        m_new = m_prev
        for s in scores:
            m_new = jnp.maximum(m_new, jnp.max(s, axis=-1, keepdims=True))
        corr = jnp.exp(m_prev - m_new)
        l_new = l_ref[...] * corr
        acc = acc_ref[...] * corr
        for s, v_t in zip(scores, values):
            p = jnp.exp(s - m_new)
            l_new = l_new + jnp.sum(p, axis=-1, keepdims=True)
            acc = acc + _bdot_nt(p.astype(BF16), v_t)
        l_ref[...] = l_new
        acc_ref[...] = acc
        m_ref[...] = m_new

    @pl.when(step == 0)
    def _():
        g = jnp.broadcast_to(new_token_cumsum(), (nh, q.shape[1], npad))
        own_t = _iota(g.shape, 1) == _iota(g.shape, 2)
        gq_ref[...] = jnp.sum(jnp.where(own_t, g, 0.0), axis=-1, keepdims=True)
        carry_ref[...] = jnp.zeros_like(carry_ref)
        m_ref[...] = jnp.full(m_ref.shape, NEG, F32)
        l_ref[...] = jnp.zeros_like(l_ref)
        acc_ref[...] = jnp.zeros_like(acc_ref)

    lf_all = jnp.concatenate([r[0] for r in lf_refs], axis=0)
    later = jnp.where(_iota((page, 2 * page), 1) < page,
                      (_iota((page, 2 * page), 0) > _iota((page, 2 * page), 1)).astype(F32), 1.0)
    sums = jnp.dot(lf_all, later, precision=HIGHEST, preferred_element_type=F32)
    carry = carry_ref[...]
    scores = []
    for j in reversed(range(n_pp)):
        rows = slice(j * nh, (j + 1) * nh)
        bias = sums[rows, :page] + carry
        carry = carry + sums[rows, page:]
        scores.append(_bdot_nn(q, k_refs[j][0].astype(BF16)) + _rows_as_batch(bias) + gq_ref[...])
    carry_ref[...] = carry
    absorb(scores, [v_refs[j][0].astype(BF16) for j in reversed(range(n_pp))])

    @pl.when(step == n_steps - 1)
    def _():
        s = _bdot_nn(q, kn_ref[0].astype(BF16)) + gq_ref[...] - new_token_cumsum()
        causal = _iota(s.shape, 1) >= _iota(s.shape, 2)
        absorb([jnp.where(causal, s, NEG)], [vn_ref[0].astype(BF16)])
        o_ref[0] = acc_ref[...] * (1.0 / l_ref[...])


FOX_PAGES_PER_STEP = 16


def _fox_sample(q, k_new_t, v_new_t, lf_new_t, cache_k_t, cache_v_t, cache_lf_t, page_table, layer_off):
    db, nh, t_new, dh = q.shape
    n_pages = page_table.shape[1]
    page = cache_k_t.shape[-1]
    n_pp = FOX_PAGES_PER_STEP
    assert n_pages % n_pp == 0
    n_steps = n_pages // n_pp
    per_b = lambda a: pl.BlockSpec((1,) + a.shape[1:], lambda b, i, pt: (b,) + (0,) * (a.ndim - 1))

    def paged(a, j):
        return pl.BlockSpec((1,) + a.shape[1:],
                            lambda b, i, pt: (layer_off + pt[b, n_pages - n_pp * (i + 1) + j],) + (0,) * (a.ndim - 1))

    caches = (cache_k_t, cache_v_t, cache_lf_t)
    return pl.pallas_call(
        functools.partial(_fox_sample_kernel, n_pp=n_pp, n_steps=n_steps),
        grid_spec=pltpu.PrefetchScalarGridSpec(
            num_scalar_prefetch=1,
            grid=(db, n_steps),
            in_specs=[per_b(q), per_b(k_new_t), per_b(v_new_t), per_b(lf_new_t)]
                     + [paged(a, j) for a in caches for j in range(n_pp)],
            out_specs=pl.BlockSpec((1, nh, t_new, dh), lambda b, i, pt: (b, 0, 0, 0)),
            scratch_shapes=[pltpu.VMEM((nh, t_new, 1), F32), pltpu.VMEM((nh, page), F32),
                            pltpu.VMEM((nh, t_new, 1), F32), pltpu.VMEM((nh, t_new, 1), F32),
                            pltpu.VMEM((nh, t_new, dh), F32)]),
        out_shape=jax.ShapeDtypeStruct((db, nh, t_new, dh), F32),
        compiler_params=_params(("arbitrary", "arbitrary"), 52),
        name="fox_sample",
    )(page_table, q, k_new_t, v_new_t, lf_new_t, *[a for a in caches for _ in range(n_pp)])


def _split_cols(w, sizes):
    out, a = [], 0
    for s in sizes:
        out.append(w[:, a:a + s])
        a += s
    return out


def _ab_weights(w_in, w_a2, b_a2):
    aw = len(A_BRANCHES) * A_BW
    aq, ak, av, bq, bk, bv, br, ba = _split_cols(w_in, [aw, aw, aw, B_QK, B_QK, B_OUT, B_OUT, B_GATE_RANK])
    w = {"q": aq.astype(BF16), "b": jnp.concatenate([bq, bk, bv, br], axis=1).astype(BF16),
         "ba": ba.astype(BF16), "a2": w_a2.astype(BF16), "ba2": b_a2.reshape(1, -1)}
    for g in range(len(A_BRANCHES)):
        cols = slice(g * A_BW, (g + 1) * A_BW)
        w[f"kv{g}"] = jnp.concatenate([ak[:, cols], av[:, cols]], axis=1).astype(BF16)
    return w


def _fox_weights(w_in, b_f):
    cw = C_HEADS * HEAD_DIM
    q, k, v, f = _split_cols(w_in, [cw, cw, cw, C_HEADS])
    nparts = 3
    head = jnp.arange(C_HEADS)
    selq = jnp.zeros((nparts * C_HEADS, cw), F32)
    selk = jnp.zeros((cw, nparts * C_HEADS), F32)
    oneq = jnp.zeros((1, cw), F32)
    onek = jnp.zeros((cw, 1), F32)
    for j in range(nparts):
        selq = selq.at[j * C_HEADS + head, head * HEAD_DIM + nparts + j].set(1.0)
        selk = selk.at[head * HEAD_DIM + j, j * C_HEADS + head].set(-1.0)
        oneq = oneq.at[0, head * HEAD_DIM + j].set(1.0)
        onek = onek.at[head * HEAD_DIM + nparts + j, 0].set(1.0)
    return {"q": q.astype(BF16), "kt": k.T.astype(BF16), "vt": v.T.astype(BF16),
            "f": f.astype(BF16), "ft": f.T.astype(BF16), "bf": b_f.reshape(1, -1), "bft": b_f.reshape(-1, 1),
            "selq": selq.astype(BF16), "oneq": oneq, "selk": selk.astype(BF16), "onek": onek}


NEW_TOKEN_PAD = 128


def _pad_tokens(a):
    return jnp.pad(a, ((0, 0),) * (a.ndim - 1) + ((0, NEW_TOKEN_PAD - a.shape[-1]),))


def kernel(x_prompt, x_sample, cache_a0_kv, cache_a1_kv, cache_a2_kv, state_gla, cache_c_k, cache_c_v, cache_c_logf,
           page_table, c_prompt, c_sample, w_ada, b_ada, g_mix, g_mlp, w_in_ab, w_gla_a2, b_gla_a2, g_gla_out, w_out_ab,
           w_in_fox, b_fox_f, w_out_fox, w_up, w_down, g_final):
    nb, seq, d = x_prompt.shape
    db, t_new, _ = x_sample.shape
    depth = w_ada.shape[0]
    n_pool, page = cache_c_k.shape[1], cache_c_k.shape[2]
    tm_p, tm_s = 512, db * t_new

    ada = _adaln(jnp.concatenate([c_prompt, c_sample], axis=0), w_ada, b_ada)

    def terms(layer):
        tp = [ada[layer, :nb, i * d:(i + 1) * d][:, None, :] for i in range(6)]
        ts = [jnp.repeat(ada[layer, nb:, i * d:(i + 1) * d], t_new, axis=0)[None] for i in range(6)]
        return tp, ts

    xp = x_prompt
    xs = x_sample.reshape(1, db * t_new, d)
    a_caches = (cache_a0_kv, cache_a1_kv, cache_a2_kv)
    a_p, a_s = [[], [], []], [[], [], []]
    gla_p, gla_s = [], []
    ck_p, cv_p, cf_p, ck_s, cv_s, cf_s = [], [], [], [], [], []
    g_final2 = g_final.reshape(1, d)

    for layer in range(depth):
        tp, ts = terms(layer)
        g1 = g_mix[layer].reshape(1, d)
        g2 = g_mlp[layer].reshape(1, d)
        last = layer == depth - 1
        if layer % 2 == 0:
            e = layer // 2
            w = _ab_weights(w_in_ab[e], w_gla_a2[e], b_gla_a2[e])
            w_out_a = w_out_ab[e, :A_BW].astype(BF16)
            w_out_b = w_out_ab[e, A_BW:].astype(BF16)
            g_bo = g_gla_out[e].reshape(1, B_OUT)

            q, qv1, qv2, kv0, kvv1, kv2, kvv2, bproj, log_a = _ab_in(xp, g1, tp[0], tp[1], w, tm_p)
            branch = [_dil_prompt(q, kv0, 0, q_col=0), _dil_prompt(qv1, kvv1, 1, q_col=0),
                      _dil_prompt(qv2, kvv2, 2, q_col=0)]
            o_b, st = _gla(bproj, log_a, jnp.zeros((nb, B_DV, B_QK), F32), g_bo, chunk=128, sub=8)
            mix = [o for o, _ in branch] + [l for _, l in branch]
            xp = _proj_res([o_b], [w_out_a, w_out_b], xp, tp[2], tm_p, mix=mix)
            kvs = (kv0, kvv1.reshape(nb, seq, 2 * A_BW), kv2)
            for g, (win, _) in enumerate(A_BRANCHES):
                keep = min(win, seq)
                src = kvv1[:, (seq - keep) // A_BRANCHES[1][1]:] if g == 1 else kvs[g][:, seq - keep:]
                a_p[g].append(src.reshape(nb, keep, 2, A_HEADS, HEAD_DIM))
            gla_p.append(st.reshape(nb, B_DV, B_HEADS, B_DK).transpose(0, 2, 3, 1))

            q, _, _, kv0, kvv1, kv2, _, bproj, log_a = _ab_in(xs, g1, ts[0], ts[1], w, tm_s)
            kv1 = kvv1.reshape(kv0.shape)
            q_s = q.reshape(db, t_new, len(A_BRANCHES), A_HEADS, HEAD_DIM).transpose(0, 2, 3, 1, 4)
            new_t = [a.reshape(db, t_new, 2, A_HEADS, HEAD_DIM).transpose(0, 2, 3, 4, 1) for a in (kv0, kv1, kv2)]
            bufs_t = [c[e].transpose(0, 2, 3, 4, 1) for c in a_caches]
            o_a = _dil_sample(q_s, [_pad_tokens(a) for a in new_t], bufs_t)
            o_a = o_a.transpose(0, 2, 1, 3).reshape(1, db * t_new, A_BW).astype(BF16)
            s0_t = state_gla[e].transpose(0, 3, 1, 2).reshape(db, B_DV, B_QK)
            o_b, st = _gla(bproj.reshape(db, t_new, -1), log_a.reshape(db, t_new, B_QK), s0_t, g_bo,
                           chunk=t_new, sub=16)
            xs = _proj_res([o_a, o_b.reshape(1, db * t_new, B_OUT)], [w_out_a, w_out_b], xs, ts[2], tm_s)
            for g, (win, _) in enumerate(A_BRANCHES):
                full = jnp.concatenate([bufs_t[g], new_t[g]], axis=-1)
                keep = min(win, full.shape[-1])
                a_s[g].append(full[..., full.shape[-1] - keep:].transpose(0, 4, 1, 2, 3))
            gla_s.append(st.reshape(db, B_DV, B_HEADS, B_DK).transpose(0, 2, 3, 1))
        else:
            o_idx = layer // 2
            w = _fox_weights(w_in_fox[o_idx], b_fox_f[o_idx])
            w_out = w_out_fox[o_idx].astype(BF16)
            cw = C_HEADS * HEAD_DIM

            q_aug, kt, vt, kt_aug, vt_aug, lft = _fox_in(xp, g1, tp[0], tp[1], w, tm_p)
            o = _fox_prompt(q_aug, kt_aug, vt_aug)
            xp = _proj_res([o], [w_out], xp, tp[2], tm_p)
            ck_p.append(kt.reshape(nb, C_HEADS, HEAD_DIM, seq).transpose(0, 3, 1, 2))
            cv_p.append(vt.reshape(nb, C_HEADS, HEAD_DIM, seq).transpose(0, 3, 1, 2))
            cf_p.append(lft.transpose(0, 2, 1))

            q_aug, kt, vt, _, _, lft = _fox_in(xs, g1, ts[0], ts[1], w, tm_s)
            q_s = q_aug.reshape(db, t_new, C_HEADS, 2 * HEAD_DIM)[..., :HEAD_DIM].transpose(0, 2, 1, 3)
            k_new_t = kt.reshape(C_HEADS, HEAD_DIM, db, t_new).transpose(2, 0, 1, 3)
            v_new_t = vt.reshape(C_HEADS, HEAD_DIM, db, t_new).transpose(2, 0, 1, 3)
            lf_new_t = lft.reshape(C_HEADS, db, t_new).transpose(1, 0, 2)
            o = _fox_sample(q_s, _pad_tokens(k_new_t), _pad_tokens(v_new_t), _pad_tokens(lf_new_t),
                            cache_c_k.transpose(0, 1, 3, 4, 2).reshape(-1, C_HEADS, HEAD_DIM, page),
                            cache_c_v.transpose(0, 1, 3, 4, 2).reshape(-1, C_HEADS, HEAD_DIM, page),
                            cache_c_logf.transpose(0, 1, 3, 2).reshape(-1, C_HEADS, page), page_table, o_idx * n_pool)
            o = o.transpose(0, 2, 1, 3).reshape(1, db * t_new, cw).astype(BF16)
            xs = _proj_res([o], [w_out], xs, ts[2], tm_s)
            ck_s.append(k_new_t.transpose(0, 3, 1, 2))
            cv_s.append(v_new_t.transpose(0, 3, 1, 2))
            cf_s.append(lf_new_t.transpose(0, 2, 1))

        w_up_b = w_up[layer].astype(BF16)
        w_down_b = w_down[layer].astype(BF16)
        xp = _mlp(xp, g2, tp[3], tp[4], tp[5], w_up_b, w_down_b, g_final2, tm_p, last)
        xs = _mlp(xs, g2, ts[3], ts[4], ts[5], w_up_b, w_down_b, g_final2, tm_s, last)

    stack = lambda parts: jnp.stack(parts, axis=0)
    return (xp, xs.reshape(db, t_new, d),
            stack(a_p[0]), stack(a_p[1]), stack(a_p[2]), stack(gla_p),
            stack(ck_p), stack(cv_p), stack(cf_p),
            stack(a_s[0]), stack(a_s[1]), stack(a_s[2]), stack(gla_s),
            stack(ck_s), stack(cv_s), stack(cf_s))
```

```python
import functools

import jax
import jax.numpy as jnp
from jax import lax
from jax.experimental import pallas as pl
from jax.experimental.pallas import tpu as pltpu

F32 = jnp.float32
BF16 = jnp.bfloat16
HIGHEST = lax.Precision.HIGHEST

HEAD_DIM = 64
RMS_EPS = 1e-6
A_BRANCHES = ((128, 1), (512, 4), (2048, 16))
A_HEADS = 8
A_BW = A_HEADS * HEAD_DIM
B_HEADS = 4
B_DK = 64
B_DV = 128
B_QK = B_HEADS * B_DK
B_OUT = B_HEADS * B_DV
B_GATE_RANK = 16
B_GATE_TEMP = 16.0
C_HEADS = 16
QK_SCALE = HEAD_DIM ** -0.5
NEG = -1e30

V7X_VMEM_BYTES = 64 * 1024 * 1024
LANES = 128
NT_DIMS = (((1,), (1,)), ((), ()))
TN_DIMS = (((0,), (0,)), ((), ()))


def _params(sem, vmem_mb):
    assert vmem_mb * 1024 * 1024 < V7X_VMEM_BYTES
    return pltpu.CompilerParams(dimension_semantics=sem, vmem_limit_bytes=vmem_mb * 1024 * 1024)


def _iota(shape, dim):
    return lax.broadcasted_iota(jnp.int32, shape, dim)


def _div(x, n):
    assert n & (n - 1) == 0
    return x >> (n.bit_length() - 1)


def _mod(x, n):
    assert n & (n - 1) == 0
    return x & (n - 1)


def _bdot_nn(a, b):
    return lax.dot_general(a, b, (((2,), (1,)), ((0,), (0,))), preferred_element_type=F32)


def _bdot_nt(a, b):
    return lax.dot_general(a, b, (((2,), (2,)), ((0,), (0,))), preferred_element_type=F32)


def _resident(shape):
    nd = len(shape)
    return pl.BlockSpec(shape, lambda *_: (0,) * nd, pipeline_mode=pl.Buffered(1))


def _log_sigmoid(z):
    return jnp.minimum(z, 0.0) - jnp.log1p(jnp.exp(-jnp.abs(z)))


def _modulated(x, g, shift, scale):
    var = jnp.mean(x * x, axis=-1, keepdims=True)
    h = x * lax.rsqrt(var + RMS_EPS) * g
    return h * (1.0 + scale) + shift


def _mod_spec(mod, tm):
    d = mod.shape[-1]
    if mod.shape[1] == 1:
        return pl.BlockSpec((1, 1, d), lambda g, i: (g, 0, 0))
    return pl.BlockSpec((1, tm, d), lambda g, i: (g, i, 0))


def _ada_kernel(c_ref, w_ref, b_ref, o_ref):
    o_ref[0] = jnp.dot(c_ref[...].astype(BF16), w_ref[0].astype(BF16),
                       preferred_element_type=F32) + b_ref[0]


def _adaln(c_all, w_ada, b_ada, tn=1024):
    nl, d, n = w_ada.shape
    nb = c_all.shape[0]
    return pl.pallas_call(
        _ada_kernel,
        grid=(nl, n // tn),
        in_specs=[pl.BlockSpec((nb, d), lambda l, j: (0, 0)),
                  pl.BlockSpec((1, d, tn), lambda l, j: (l, 0, j)),
                  pl.BlockSpec((1, 1, tn), lambda l, j: (l, 0, j))],
        out_specs=pl.BlockSpec((1, nb, tn), lambda l, j: (l, 0, j)),
        out_shape=jax.ShapeDtypeStruct((nl, nb, n), F32),
        compiler_params=_params(("arbitrary", "arbitrary"), 32),
        name="adaln",
    )(c_all, w_ada, b_ada.reshape(nl, 1, n))


def _ab_in_kernel(x_ref, g_ref, sh_ref, sc_ref, wq_ref, wkv0_ref, wkv1_ref, wkv2_ref, wb_ref, wba_ref,
                  wa2_ref, ba2_ref, q_ref, qv1_ref, qv2_ref, kv0_ref, kvv1_ref, kv2_ref, kvv2_ref, b_ref, la_ref,
                  stage_ref, *, tm):
    def emit_view(val, view_ref, dil):
        width = val.shape[1]
        for c in range(width // LANES):
            stage_ref[c] = val[:, c * LANES:(c + 1) * LANES]
        for r in range(dil):
            for c in range(width // LANES):
                rows = stage_ref[c, pl.ds(r, tm // dil, stride=dil), :]
                view_ref[0, :, r * width + c * LANES:r * width + (c + 1) * LANES] = rows.astype(view_ref.dtype)

    hb = _modulated(x_ref[0], g_ref[...], sh_ref[0], sc_ref[0]).astype(BF16)
    q = jnp.dot(hb, wq_ref[...], preferred_element_type=F32) * QK_SCALE
    q_ref[0] = q.astype(q_ref.dtype)
    emit_view(q[:, A_BW:2 * A_BW], qv1_ref, A_BRANCHES[1][1])
    emit_view(q[:, 2 * A_BW:], qv2_ref, A_BRANCHES[2][1])
    kv0_ref[0] = jnp.dot(hb, wkv0_ref[...], preferred_element_type=F32)
    emit_view(jnp.dot(hb, wkv1_ref[...], preferred_element_type=F32), kvv1_ref, A_BRANCHES[1][1])
    kv2 = jnp.dot(hb, wkv2_ref[...], preferred_element_type=F32)
    kv2_ref[0] = kv2
    emit_view(kv2, kvv2_ref, A_BRANCHES[2][1])
    b_ref[0] = jnp.dot(hb, wb_ref[...], preferred_element_type=F32)
    ba = jnp.dot(hb, wba_ref[...], preferred_element_type=F32)
    z = jnp.dot(ba.astype(BF16), wa2_ref[...], preferred_element_type=F32) + ba2_ref[...]
    la_ref[0] = _log_sigmoid(z) * (1.0 / B_GATE_TEMP)


def _ab_in(x, g, shift, scale, w, tm):
    ng, t, d = x.shape
    d1, d2 = A_BRANCHES[1][1], A_BRANCHES[2][1]
    assert A_BRANCHES[0][1] == 1 and tm % (8 * d2) == 0
    sds = jax.ShapeDtypeStruct
    tok = lambda n: pl.BlockSpec((1, tm, n), lambda gi, i: (gi, i, 0))
    view = lambda n, dil: pl.BlockSpec((1, tm // dil, dil * n), lambda gi, i: (gi, i, 0))
    weights = [w["q"], w["kv0"], w["kv1"], w["kv2"], w["b"], w["ba"], w["a2"], w["ba2"]]
    nb_w = 2 * B_QK + 2 * B_OUT
    return pl.pallas_call(
        functools.partial(_ab_in_kernel, tm=tm),
        grid=(ng, t // tm),
        in_specs=[tok(d), pl.BlockSpec((1, d), lambda gi, i: (0, 0)), _mod_spec(shift, tm), _mod_spec(scale, tm)]
                 + [_resident(a.shape) for a in weights],
        out_specs=[tok(3 * A_BW), view(A_BW, d1), view(A_BW, d2), tok(2 * A_BW), view(2 * A_BW, d1),
                   tok(2 * A_BW), view(2 * A_BW, d2), tok(nb_w), tok(B_QK)],
        out_shape=[sds((ng, t, 3 * A_BW), BF16), sds((ng, t // d1, d1 * A_BW), BF16), sds((ng, t // d2, d2 * A_BW), BF16),
                   sds((ng, t, 2 * A_BW), F32), sds((ng, t // d1, d1 * 2 * A_BW), F32),
                   sds((ng, t, 2 * A_BW), F32), sds((ng, t // d2, d2 * 2 * A_BW), F32),
                   sds((ng, t, nb_w), F32), sds((ng, t, B_QK), F32)],
        scratch_shapes=[pltpu.VMEM((2 * A_BW // LANES, tm, LANES), F32)],
        compiler_params=_params(("arbitrary", "arbitrary"), 58),
        name="ab_in_proj",
    )(x, g, shift, scale, *weights)


DIL_BLOCKS_PER_STEP = 4


def _dil_prompt_kernel(q_ref, kc_ref, vc_ref, *refs, span, nsub, single_block):
    if single_block:
        _dil_single_block(q_ref, kc_ref, vc_ref, *refs, span=span)
        return
    kp_ref, vp_ref, o_ref, lse_ref = refs
    n = pl.program_id(2)
    q = q_ref[0]
    kc = kc_ref[0].astype(BF16)
    vc = vc_ref[0].astype(BF16)
    kp = kp_ref[0].astype(BF16)
    vp = vp_ref[0].astype(BF16)
    qi = _iota((span, span), 0)
    kj = _iota((span, span), 1)
    cur_ok = kj <= qi
    prev_ok = kj >= qi
    first_off = jnp.where(n > 0, 0.0, NEG)
    units = []
    for j in range(nsub):
        rows = slice(j * span, (j + 1) * span)
        k_prev, v_prev = (kp, vp) if j == 0 else (kc[(j - 1) * span:j * span], vc[(j - 1) * span:j * span])
        for h in range(A_HEADS):
            units.append((rows, slice(h * HEAD_DIM, (h + 1) * HEAD_DIM), k_prev, v_prev, j == 0))
    scores = [(lax.dot_general(q[rows, sl], kc[rows, sl], NT_DIMS, preferred_element_type=F32),
               lax.dot_general(q[rows, sl], k_prev[:, sl], NT_DIMS, preferred_element_type=F32))
              for rows, sl, k_prev, _, _ in units]
    probs, lses = [], []
    for (sc, sp), (_, _, _, _, is_first) in zip(scores, units):
        sc = jnp.where(cur_ok, sc, NEG)
        sp = jnp.where(prev_ok, sp + first_off if is_first else sp, NEG)
        m = jnp.maximum(jnp.max(sc, axis=-1, keepdims=True), jnp.max(sp, axis=-1, keepdims=True))
        pc = jnp.exp(sc - m)
        pp = jnp.exp(sp - m)
        den = jnp.sum(pc, axis=-1, keepdims=True) + jnp.sum(pp, axis=-1, keepdims=True)
        inv = 1.0 / den
        probs.append(((pc * inv).astype(BF16), (pp * inv).astype(BF16)))
        lses.append(jnp.broadcast_to(m + jnp.log(den), (span, HEAD_DIM)))
    outs = [jnp.dot(pc, vc[rows, sl], preferred_element_type=F32) + jnp.dot(pp, v_prev[:, sl], preferred_element_type=F32)
            for (pc, pp), (rows, sl, _, v_prev, _) in zip(probs, units)]
    for j in range(nsub):
        rows = slice(j * span, (j + 1) * span)
        o_ref[0, rows, :] = jnp.concatenate(outs[j * A_HEADS:(j + 1) * A_HEADS], axis=1)
        lse_ref[0, rows, :] = jnp.concatenate(lses[j * A_HEADS:(j + 1) * A_HEADS], axis=1)


def _dil_single_block(q_ref, k_ref, v_ref, o_ref, lse_ref, *, span):
    q = q_ref[0]
    k = k_ref[0].astype(BF16)
    v = v_ref[0].astype(BF16)
    causal = _iota((span, span), 1) <= _iota((span, span), 0)
    heads = [slice(h * HEAD_DIM, (h + 1) * HEAD_DIM) for h in range(A_HEADS)]
    scores = [lax.dot_general(q[:, sl], k[:, sl], NT_DIMS, preferred_element_type=F32) for sl in heads]
    probs, lses = [], []
    for s in scores:
        s = jnp.where(causal, s, NEG)
        m = jnp.max(s, axis=-1, keepdims=True)
        p = jnp.exp(s - m)
        den = jnp.sum(p, axis=-1, keepdims=True)
        probs.append((p * (1.0 / den)).astype(BF16))
        lses.append(jnp.broadcast_to(m + jnp.log(den), (span, HEAD_DIM)))
    outs = [jnp.dot(p, v[:, sl], preferred_element_type=F32) for p, sl in zip(probs, heads)]
    o_ref[0] = jnp.concatenate(outs, axis=1)
    lse_ref[0] = jnp.concatenate(lses, axis=1)


def _dil_prompt(q_v, kv_v, branch, q_col):
    win, dil = A_BRANCHES[branch]
    nbatch, rows, _ = kv_v.shape
    span = win // dil
    assert kv_v.shape[2] == dil * 2 * A_BW and rows % span == 0
    nq = q_v.shape[2] // (dil * A_BW)
    nsub = DIL_BLOCKS_PER_STEP if (rows // span) % DIL_BLOCKS_PER_STEP == 0 else 1
    nblk = rows // (span * nsub)
    blk = (1, nsub * span, A_BW)
    before = (1, span, A_BW)
    out_sds = jax.ShapeDtypeStruct((nbatch, rows, dil * A_BW), F32)
    single_block = rows == span
    prev_specs = [] if single_block else [
        pl.BlockSpec(before, lambda b, r, n: (b, jnp.maximum(nsub * n - 1, 0), 2 * r)),
        pl.BlockSpec(before, lambda b, r, n: (b, jnp.maximum(nsub * n - 1, 0), 2 * r + 1))]
    return pl.pallas_call(
        functools.partial(_dil_prompt_kernel, span=span, nsub=nsub, single_block=single_block),
        grid=(nbatch, dil, nblk),
        in_specs=[pl.BlockSpec(blk, lambda b, r, n: (b, n, nq * r + q_col)),
                  pl.BlockSpec(blk, lambda b, r, n: (b, n, 2 * r)),
                  pl.BlockSpec(blk, lambda b, r, n: (b, n, 2 * r + 1))] + prev_specs,
        out_specs=[pl.BlockSpec(blk, lambda b, r, n: (b, n, r))] * 2,
        out_shape=[out_sds, out_sds],
        compiler_params=_params(("arbitrary",) * 3, 32),
        name=f"dilated_prompt_{branch}",
    )(q_v, kv_v, kv_v, *([kv_v] * len(prev_specs)))


def _dil_sample_kernel(q_ref, new0_ref, new1_ref, new2_ref, buf0_ref, buf1_ref, buf2_ref, o_ref):
    outs, lses = [], []
    for g, (new_ref, buf_ref) in enumerate(((new0_ref, buf0_ref), (new1_ref, buf1_ref), (new2_ref, buf2_ref))):
        win, dil = A_BRANCHES[g]
        lw = buf_ref.shape[-1]
        npad = new_ref.shape[-1]
        q = q_ref[0, g]
        t_new = q.shape[1]
        s1 = _bdot_nn(q, buf_ref[0, 0].astype(BF16))
        s2 = _bdot_nn(q, new_ref[0, 0].astype(BF16))
        delta1 = lw + _iota(s1.shape, 1) - _iota(s1.shape, 2)
        delta2 = _iota(s2.shape, 1) - _iota(s2.shape, 2)
        s1 = jnp.where((_mod(delta1, dil) == 0) & (delta1 <= win), s1, NEG)
        s2 = jnp.where((_mod(delta2, dil) == 0) & (delta2 >= 0), s2, NEG)
        m = jnp.maximum(jnp.max(s1, axis=-1, keepdims=True), jnp.max(s2, axis=-1, keepdims=True))
        p1 = jnp.exp(s1 - m)
        p2 = jnp.exp(s2 - m)
        den = jnp.sum(p1, axis=-1, keepdims=True) + jnp.sum(p2, axis=-1, keepdims=True)
        inv = 1.0 / den
        o = _bdot_nt((p1 * inv).astype(BF16), buf_ref[0, 1].astype(BF16))
        o = o + _bdot_nt((p2 * inv).astype(BF16), new_ref[0, 1].astype(BF16))
        outs.append(o)
        lses.append(m + jnp.log(den))
    mx = jnp.maximum(jnp.maximum(lses[0], lses[1]), lses[2])
    es = [jnp.exp(l - mx) for l in lses]
    inv = 1.0 / (es[0] + es[1] + es[2])
    o_ref[0] = (es[0] * outs[0] + es[1] * outs[1] + es[2] * outs[2]) * inv


def _dil_sample(q, new_t, bufs_t):
    db, _, nh, t_new, dh = q.shape
    for (win, _), buf in zip(A_BRANCHES, bufs_t):
        assert buf.shape[-1] == win
    per_b = lambda a: pl.BlockSpec((1,) + a.shape[1:], lambda b: (b,) + (0,) * (a.ndim - 1))
    return pl.pallas_call(
        _dil_sample_kernel,
        grid=(db,),
        in_specs=[per_b(q)] + [per_b(a) for a in new_t] + [per_b(a) for a in bufs_t],
        out_specs=pl.BlockSpec((1, nh, t_new, dh), lambda b: (b, 0, 0, 0)),
        out_shape=jax.ShapeDtypeStruct((db, nh, t_new, dh), F32),
        compiler_params=_params(("arbitrary",), 48),
        name="dilated_sample",
    )(q, *new_t, *bufs_t)


def _gla_kernel(qk_ref, v_ref, r_ref, la_ref, s0_ref, gbo_ref, o_ref, st_ref, *, chunk, sub):
    @pl.when(pl.program_id(1) == 0)
    def _():
        st_ref[0] = s0_ref[0]

    q = qk_ref[0, :, :B_QK] * (B_DK ** -0.5)
    k = qk_ref[0, :, B_QK:]
    v = v_ref[0]
    la = la_ref[0]
    rows_in = chunk
    if chunk < sub:
        grow = lambda a: jnp.concatenate([a, jnp.zeros((sub - chunk, a.shape[1]), F32)], axis=0)
        q, k, v, la = grow(q), grow(k), grow(v), grow(la)
        chunk = sub
    tri = (_iota((chunk, chunk), 0) >= _iota((chunk, chunk), 1)).astype(F32)
    cum = jnp.dot(tri, la, precision=HIGHEST, preferred_element_type=F32)
    last = cum[chunk - 1:chunk, :]
    st = st_ref[0]
    qe = (q * jnp.exp(cum)).astype(BF16)
    kd = (k * jnp.exp(last - cum)).astype(BF16)
    vb = v.astype(BF16)
    heads = [(slice(h * B_DK, (h + 1) * B_DK), slice(h * B_DV, (h + 1) * B_DV)) for h in range(B_HEADS)]
    st_b = st.astype(BF16)
    inter = [lax.dot_general(qe[:, ks], st_b[:, ks], NT_DIMS, preferred_element_type=F32) for ks, _ in heads]
    upd = [lax.dot_general(vb[:, vs], kd[:, ks], TN_DIMS, preferred_element_type=F32) for ks, vs in heads]
    starts = list(range(sub, chunk, sub))
    factored = []
    for r0 in starts:
        ref = cum[r0:r0 + 1, :]
        factored.append(((q[r0:r0 + sub] * jnp.exp(cum[r0:r0 + sub] - ref)).astype(BF16),
                         (k[:r0] * jnp.exp(ref - cum[:r0])).astype(BF16)))
    atts = [[lax.dot_general(qt[:, ks], kt[:, ks], NT_DIMS, preferred_element_type=F32).astype(BF16)
             for qt, kt in factored] for ks, _ in heads]
    o_heads = []
    for (_, vs), inter_h, att_h in zip(heads, inter, atts):
        blocks = [jnp.zeros((sub, B_DV), F32)]
        blocks += [jnp.dot(a, vb[:r0, vs], preferred_element_type=F32) for a, r0 in zip(att_h, starts)]
        o_heads.append(inter_h + jnp.concatenate(blocks, axis=0) if starts else inter_h)
    st_ref[0] = st * jnp.exp(last) + jnp.concatenate(upd, axis=1)
    o = jnp.concatenate(o_heads, axis=1)

    spread = (_div(_iota((B_QK, B_OUT), 0), B_DK) == _div(_iota((B_QK, B_OUT), 1), B_DV)).astype(BF16)
    row_in_block = _mod(_iota((chunk, B_QK), 0), sub)
    for off in range(sub):
        k_s = k if off == 0 else pltpu.roll(k, off, 0)
        c_s = cum if off == 0 else pltpu.roll(cum, off, 0)
        v_s = v if off == 0 else pltpu.roll(v, off, 0)
        term = q * k_s * jnp.exp(jnp.minimum(cum - c_s, 0.0))
        term = jnp.where(row_in_block >= off, term, 0.0)
        att = jnp.dot(term.astype(BF16), spread, preferred_element_type=F32)
        o = o + att * v_s

    o = o[:rows_in]
    r = r_ref[0]
    gate = r * (1.0 / (1.0 + jnp.exp(-r)))
    normed = []
    for h in range(B_HEADS):
        vs = slice(h * B_DV, (h + 1) * B_DV)
        o_h = o[:, vs]
        var = jnp.mean(o_h * o_h, axis=-1, keepdims=True)
        normed.append(o_h * lax.rsqrt(var + RMS_EPS))
    o_ref[0] = (jnp.concatenate(normed, axis=1) * gbo_ref[...] * gate).astype(o_ref.dtype)


def _gla(bproj, log_a, s0_t, g_bo, chunk, sub):
    nb, t, _ = bproj.shape
    assert t % chunk == 0 and (chunk % sub == 0 or (chunk < sub and t == chunk))
    half = 2 * B_QK
    assert half == B_OUT
    col = lambda j: pl.BlockSpec((1, chunk, half), lambda b, c: (b, c, j))
    st_spec = pl.BlockSpec((1, B_DV, B_QK), lambda b, c: (b, 0, 0))
    return pl.pallas_call(
        functools.partial(_gla_kernel, chunk=chunk, sub=sub),
        grid=(nb, t // chunk),
        in_specs=[col(0), col(1), col(2),
                  pl.BlockSpec((1, chunk, B_QK), lambda b, c: (b, c, 0)),
                  st_spec,
                  pl.BlockSpec((1, B_OUT), lambda b, c: (0, 0))],
        out_specs=[pl.BlockSpec((1, chunk, B_OUT), lambda b, c: (b, c, 0)), st_spec],
        out_shape=[jax.ShapeDtypeStruct((nb, t, B_OUT), BF16), jax.ShapeDtypeStruct((nb, B_DV, B_QK), F32)],
        compiler_params=_params(("arbitrary", "arbitrary"), 32),
        name=f"gla_chunk{chunk}",
    )(bproj, bproj, bproj, log_a, s0_t, g_bo)


def _proj_res_kernel(*refs, n_in, mix3):
    if mix3:
        nat_ref, refs = refs[-1], refs[:-1]
        tm = nat_ref.shape[1]
        ncol = A_BW // LANES
        slot = 0
        nat = []
        for ref in refs[:6]:
            dil = ref.shape[2] // A_BW
            if dil == 1:
                nat.append(ref[0])
                continue
            for r in range(dil):
                for c in range(ncol):
                    lanes = slice(r * A_BW + c * LANES, r * A_BW + (c + 1) * LANES)
                    nat_ref[slot * ncol + c, pl.ds(r, tm // dil, stride=dil), :] = ref[0, :, lanes]
            nat.append(jnp.concatenate([nat_ref[slot * ncol + c] for c in range(ncol)], axis=1))
            slot += 1
        o0, o1, o2, l0, l1, l2 = nat
        refs = refs[6:]
        mx = jnp.maximum(jnp.maximum(l0, l1), l2)
        e0, e1, e2 = jnp.exp(l0 - mx), jnp.exp(l1 - mx), jnp.exp(l2 - mx)
        first = ((e0 * o0 + e1 * o1 + e2 * o2) * (1.0 / (e0 + e1 + e2))).astype(BF16)
        ins = [first] + [r[0] for r in refs[:n_in - 1]]
        refs = refs[n_in - 1:]
    else:
        ins = [r[0] for r in refs[:n_in]]
        refs = refs[n_in:]
    w_refs, (x_ref, gate_ref, out_ref) = refs[:n_in], refs[n_in:]
    y = jnp.dot(ins[0], w_refs[0][...], preferred_element_type=F32)
    for a, w_ref in zip(ins[1:], w_refs[1:]):
        y = y + jnp.dot(a, w_ref[...], preferred_element_type=F32)
    out_ref[0] = x_ref[0] + gate_ref[0] * y


def _proj_res(ins, ws, x, gate, tm, mix=None):
    ng, t, d = x.shape
    tok = lambda a: pl.BlockSpec((1, tm, a.shape[-1]), lambda gi, i: (gi, i, 0))
    view = lambda a: pl.BlockSpec((1, tm * a.shape[1] // t, a.shape[2]), lambda gi, i: (gi, i, 0))
    lead = list(mix) if mix is not None else []
    n_in = len(ws)
    n_views = sum(a.shape[1] != t for a in lead)
    return pl.pallas_call(
        functools.partial(_proj_res_kernel, n_in=n_in, mix3=mix is not None),
        grid=(ng, t // tm),
        in_specs=[view(a) for a in lead] + [tok(a) for a in ins] + [_resident(w.shape) for w in ws]
                 + [tok(x), _mod_spec(gate, tm)],
        out_specs=tok(x),
        out_shape=jax.ShapeDtypeStruct(x.shape, F32),
        scratch_shapes=[pltpu.VMEM((n_views * A_BW // LANES, tm, LANES), F32)] if mix is not None else [],
        compiler_params=_params(("arbitrary", "arbitrary"), 48),
        name="proj_residual_mix" if mix is not None else "proj_residual",
    )(*lead, *ins, *ws, x, gate)


def _mlp_kernel(x_ref, g_ref, sh_ref, sc_ref, gate_ref, wu_ref, wd_ref, gf_ref, out_ref, *, fchunk, final_norm):
    x = x_ref[0]
    hb = _modulated(x, g_ref[...], sh_ref[0], sc_ref[0]).astype(BF16)
    acc = jnp.zeros(x.shape, F32)
    for f0 in range(0, wu_ref.shape[1], fchunk):
        u = jnp.maximum(jnp.dot(hb, wu_ref[:, f0:f0 + fchunk], preferred_element_type=F32), 0.0)
        acc = acc + jnp.dot((u * u).astype(BF16), wd_ref[f0:f0 + fchunk, :], preferred_element_type=F32)
    y = x + gate_ref[0] * acc
    if final_norm:
        var = jnp.mean(y * y, axis=-1, keepdims=True)
        y = y * lax.rsqrt(var + RMS_EPS) * gf_ref[...]
    out_ref[0] = y


def _mlp(x, g, shift, scale, gate, w_up, w_down, g_final, tm, final_norm):
    ng, t, d = x.shape
    tok = pl.BlockSpec((1, tm, d), lambda gi, i: (gi, i, 0))
    vec = pl.BlockSpec((1, d), lambda gi, i: (0, 0))
    return pl.pallas_call(
        functools.partial(_mlp_kernel, fchunk=1024, final_norm=final_norm),
        grid=(ng, t // tm),
        in_specs=[tok, vec, _mod_spec(shift, tm), _mod_spec(scale, tm), _mod_spec(gate, tm),
                  _resident(w_up.shape), _resident(w_down.shape), vec],
        out_specs=tok,
        out_shape=jax.ShapeDtypeStruct(x.shape, F32),
        compiler_params=_params(("arbitrary", "arbitrary"), 56),
        name="mlp_final" if final_norm else "mlp",
    )(x, g, shift, scale, gate, w_up, w_down, g_final)


def _bf16_parts(c):
    c1 = c.astype(BF16).astype(F32)
    c2 = (c - c1).astype(BF16).astype(F32)
    c3 = (c - c1 - c2).astype(BF16).astype(F32)
    return c1, c2, c3


def _fox_in_kernel(x_ref, g_ref, sh_ref, sc_ref, wq_ref, wkt_ref, wvt_ref, wf_ref, wft_ref, bf_ref, bft_ref,
                   selq_ref, oneq_ref, selk_ref, onek_ref,
                   qa_ref, kt_ref, vt_ref, kta_ref, vta_ref, lft_ref, carry_row_ref, carry_col_ref, *, tm):
    @pl.when(pl.program_id(1) == 0)
    def _():
        carry_row_ref[...] = jnp.zeros_like(carry_row_ref)
        carry_col_ref[...] = jnp.zeros_like(carry_col_ref)

    dh = HEAD_DIM
    hb = _modulated(x_ref[0], g_ref[...], sh_ref[0], sc_ref[0]).astype(BF16)
    q = jnp.dot(hb, wq_ref[...], preferred_element_type=F32) * QK_SCALE
    kt = lax.dot_general(wkt_ref[...], hb, NT_DIMS, preferred_element_type=F32)
    vt = lax.dot_general(wvt_ref[...], hb, NT_DIMS, preferred_element_type=F32)
    kt_ref[0] = kt
    vt_ref[0] = vt
    lf = _log_sigmoid(jnp.dot(hb, wf_ref[...], preferred_element_type=F32) + bf_ref[...])
    lft = _log_sigmoid(lax.dot_general(wft_ref[...], hb, NT_DIMS, preferred_element_type=F32) + bft_ref[...])
    lft_ref[0] = lft
    upto = (_iota((tm, tm), 0) <= _iota((tm, tm), 1)).astype(F32)
    since = (_iota((tm, tm), 0) >= _iota((tm, tm), 1)).astype(F32)
    cumq = jnp.dot(since, lf, precision=HIGHEST, preferred_element_type=F32) + carry_row_ref[...]
    carry_row_ref[...] = cumq[tm - 1:tm, :]
    cumt = jnp.dot(lft, upto, precision=HIGHEST, preferred_element_type=F32) + carry_col_ref[:, :1]
    carry_col_ref[...] = jnp.broadcast_to(cumt[:, tm - 1:tm], carry_col_ref.shape)

    parts_q = jnp.concatenate(_bf16_parts(cumq), axis=1).astype(BF16)
    bias_q = jnp.dot(parts_q, selq_ref[...], preferred_element_type=F32) + oneq_ref[...]
    parts_k = jnp.concatenate(_bf16_parts(cumt), axis=0).astype(BF16)
    bias_k = jnp.dot(selk_ref[...], parts_k, preferred_element_type=F32) + onek_ref[...]
    ones = jnp.ones((dh, tm), F32)
    qa, kta, vta = [], [], []
    for h in range(C_HEADS):
        sl = slice(h * dh, (h + 1) * dh)
        qa += [q[:, sl], bias_q[:, sl]]
        kta += [kt[sl], bias_k[sl]]
        vta += [vt[sl], ones]
    qa_ref[0] = jnp.concatenate(qa, axis=1).astype(BF16)
    kta_ref[0] = jnp.concatenate(kta, axis=0).astype(BF16)
    vta_ref[0] = jnp.concatenate(vta, axis=0).astype(BF16)


def _fox_in(x, g, shift, scale, w, tm):
    ng, t, d = x.shape
    cw = C_HEADS * HEAD_DIM
    tok = lambda n: pl.BlockSpec((1, tm, n), lambda gi, i: (gi, i, 0))
    tok_t = lambda n: pl.BlockSpec((1, n, tm), lambda gi, i: (gi, 0, i))
    weights = [w["q"], w["kt"], w["vt"], w["f"], w["ft"], w["bf"], w["bft"], w["selq"], w["oneq"], w["selk"], w["onek"]]
    sds = jax.ShapeDtypeStruct
    return pl.pallas_call(
        functools.partial(_fox_in_kernel, tm=tm),
        grid=(ng, t // tm),
        in_specs=[tok(d), pl.BlockSpec((1, d), lambda gi, i: (0, 0)), _mod_spec(shift, tm), _mod_spec(scale, tm)]
                 + [_resident(a.shape) for a in weights],
        out_specs=[tok(2 * cw), tok_t(cw), tok_t(cw), tok_t(2 * cw), tok_t(2 * cw), tok_t(C_HEADS)],
        out_shape=[sds((ng, t, 2 * cw), BF16), sds((ng, cw, t), F32), sds((ng, cw, t), F32),
                   sds((ng, 2 * cw, t), BF16), sds((ng, 2 * cw, t), BF16), sds((ng, C_HEADS, t), F32)],
        scratch_shapes=[pltpu.VMEM((1, C_HEADS), F32), pltpu.VMEM((C_HEADS, 128), F32)],
        compiler_params=_params(("arbitrary", "arbitrary"), 56),
        name="fox_in_proj",
    )(x, g, shift, scale, *weights)


FOX_TQ = 512
FOX_HEAD_GROUP = 4


def _fox_prompt_kernel(q_ref, kt_ref, vt_ref, o_ref, m_ref, acc_ref):
    qi = pl.program_id(1)
    ki = pl.program_id(2)
    tq = FOX_TQ
    hw = 2 * HEAD_DIM

    @pl.when(ki == 0)
    def _():
        m_ref[...] = jnp.full(m_ref.shape, NEG, F32)
        acc_ref[...] = jnp.zeros_like(acc_ref)

    def update(diagonal):
        causal = _iota((tq, tq), 0) >= _iota((tq, tq), 1)
        for h0 in range(0, C_HEADS, FOX_HEAD_GROUP):
            group = range(h0, h0 + FOX_HEAD_GROUP)
            lanes = [slice(h * hw, (h + 1) * hw) for h in group]
            scores = [jnp.dot(q_ref[0, :, sl], kt_ref[0, sl, :], preferred_element_type=F32) for sl in lanes]
            probs, corrs = [], []
            for h, s in zip(group, scores):
                if diagonal:
                    s = jnp.where(causal, s, NEG)
                m_prev = m_ref[h]
                m_new = jnp.maximum(m_prev, jnp.max(s, axis=-1, keepdims=True))
                m_ref[h] = m_new
                corrs.append(jnp.exp(m_prev - m_new))
                probs.append(jnp.exp(s - jnp.concatenate([m_new] * (tq // hw), axis=1)).astype(BF16))
            pvs = [lax.dot_general(p, vt_ref[0, sl, :], NT_DIMS, preferred_element_type=F32)
                   for p, sl in zip(probs, lanes)]
            accs = [acc_ref[:, sl] * corr + pv for sl, corr, pv in zip(lanes, corrs, pvs)]
            if diagonal:
                outs = [acc * (1.0 / pltpu.roll(acc, HEAD_DIM, 1)) for acc in accs]
                for j in range(0, FOX_HEAD_GROUP, 2):
                    pair = jnp.concatenate([outs[j][:, :HEAD_DIM], outs[j + 1][:, :HEAD_DIM]], axis=1)
                    o_ref[0, :, (h0 + j) * HEAD_DIM:(h0 + j + 2) * HEAD_DIM] = pair.astype(o_ref.dtype)
            else:
                for sl, acc in zip(lanes, accs):
                    acc_ref[:, sl] = acc

    @pl.when(ki < qi)
    def _():
        update(False)

    @pl.when(ki == qi)
    def _():
        update(True)


def _fox_prompt(q_aug, kt_aug, vt_aug):
    nb, s, w2 = q_aug.shape
    tq = FOX_TQ
    nq = s // tq
    assert 2 * HEAD_DIM == 128 and FOX_HEAD_GROUP % 2 == 0 and C_HEADS % FOX_HEAD_GROUP == 0
    kv_spec = pl.BlockSpec((1, w2, tq), lambda b, i, j: (b, 0, jnp.minimum(i, j)))
    return pl.pallas_call(
        _fox_prompt_kernel,
        grid=(nb, nq, nq),
        in_specs=[pl.BlockSpec((1, tq, w2), lambda b, i, j: (b, i, 0)), kv_spec, kv_spec],
        out_specs=pl.BlockSpec((1, tq, w2 // 2), lambda b, i, j: (b, i, 0)),
        out_shape=jax.ShapeDtypeStruct((nb, s, w2 // 2), BF16),
        scratch_shapes=[pltpu.VMEM((C_HEADS, tq, 2 * HEAD_DIM), F32), pltpu.VMEM((tq, w2), F32)],
        compiler_params=_params(("arbitrary",) * 3, 48),
        name="fox_prompt",
    )(q_aug, kt_aug, vt_aug)


def _rows_as_batch(x):
    return jnp.concatenate([x[h:h + 1][None] for h in range(x.shape[0])], axis=0)


def _fox_sample_kernel(pt_ref, q_ref, kn_ref, vn_ref, lfn_ref, *refs, n_pp, n_steps):
    del pt_ref
    k_refs, v_refs, lf_refs = refs[:n_pp], refs[n_pp:2 * n_pp], refs[2 * n_pp:3 * n_pp]
    o_ref, gq_ref, carry_ref, m_ref, l_ref, acc_ref = refs[3 * n_pp:]
    step = pl.program_id(1)
    q = q_ref[0]
    nh = q.shape[0]
    page = lf_refs[0].shape[-1]
    npad = lfn_ref.shape[-1]

    def new_token_cumsum():
        upto = (_iota((npad, npad), 0) <= _iota((npad, npad), 1)).astype(F32)
        return _rows_as_batch(jnp.dot(lfn_ref[0], upto, precision=HIGHEST, preferred_element_type=F32))

    def absorb(scores, values):
        m_prev = m_ref[...]
        m_new = m_prev
        for s in scores:
            m_new = jnp.maximum(m_new, jnp.max(s, axis=-1, keepdims=True))
        corr = jnp.exp(m_prev - m_new)
        l_new = l_ref[...] * corr
        acc = acc_ref[...] * corr
        for s, v_t in zip(scores, values):
            p = jnp.exp(s - m_new)
            l_new = l_new + jnp.sum(p, axis=-1, keepdims=True)
            acc = acc + _bdot_nt(p.astype(BF16), v_t)
        l_ref[...] = l_new
        acc_ref[...] = acc
        m_ref[...] = m_new

    @pl.when(step == 0)
    def _():
        g = jnp.broadcast_to(new_token_cumsum(), (nh, q.shape[1], npad))
        own_t = _iota(g.shape, 1) == _iota(g.shape, 2)
        gq_ref[...] = jnp.sum(jnp.where(own_t, g, 0.0), axis=-1, keepdims=True)
        carry_ref[...] = jnp.zeros_like(carry_ref)
        m_ref[...] = jnp.full(m_ref.shape, NEG, F32)
        l_ref[...] = jnp.zeros_like(l_ref)
        acc_ref[...] = jnp.zeros_like(acc_ref)

    lf_all = jnp.concatenate([r[0] for r in lf_refs], axis=0)
    later = jnp.where(_iota((page, 2 * page), 1) < page,
                      (_iota((page, 2 * page), 0) > _iota((page, 2 * page), 1)).astype(F32), 1.0)
    sums = jnp.dot(lf_all, later, precision=HIGHEST, preferred_element_type=F32)
    carry = carry_ref[...]
    scores = []
    for j in reversed(range(n_pp)):
        rows = slice(j * nh, (j + 1) * nh)
        bias = sums[rows, :page] + carry
        carry = carry + sums[rows, page:]
        scores.append(_bdot_nn(q, k_refs[j][0].astype(BF16)) + _rows_as_batch(bias) + gq_ref[...])
    carry_ref[...] = carry
    absorb(scores, [v_refs[j][0].astype(BF16) for j in reversed(range(n_pp))])

    @pl.when(step == n_steps - 1)
    def _():
        s = _bdot_nn(q, kn_ref[0].astype(BF16)) + gq_ref[...] - new_token_cumsum()
        causal = _iota(s.shape, 1) >= _iota(s.shape, 2)
        absorb([jnp.where(causal, s, NEG)], [vn_ref[0].astype(BF16)])
        o_ref[0] = acc_ref[...] * (1.0 / l_ref[...])


FOX_PAGES_PER_STEP = 16


def _fox_sample(q, k_new_t, v_new_t, lf_new_t, cache_k_t, cache_v_t, cache_lf_t, page_table, layer_off):
    db, nh, t_new, dh = q.shape
    n_pages = page_table.shape[1]
    page = cache_k_t.shape[-1]
    n_pp = FOX_PAGES_PER_STEP
    assert n_pages % n_pp == 0
    n_steps = n_pages // n_pp
    per_b = lambda a: pl.BlockSpec((1,) + a.shape[1:], lambda b, i, pt: (b,) + (0,) * (a.ndim - 1))

    def paged(a, j):
        return pl.BlockSpec((1,) + a.shape[1:],
                            lambda b, i, pt: (layer_off + pt[b, n_pages - n_pp * (i + 1) + j],) + (0,) * (a.ndim - 1))

    caches = (cache_k_t, cache_v_t, cache_lf_t)
    return pl.pallas_call(
        functools.partial(_fox_sample_kernel, n_pp=n_pp, n_steps=n_steps),
        grid_spec=pltpu.PrefetchScalarGridSpec(
            num_scalar_prefetch=1,
            grid=(db, n_steps),
            in_specs=[per_b(q), per_b(k_new_t), per_b(v_new_t), per_b(lf_new_t)]
                     + [paged(a, j) for a in caches for j in range(n_pp)],
            out_specs=pl.BlockSpec((1, nh, t_new, dh), lambda b, i, pt: (b, 0, 0, 0)),
            scratch_shapes=[pltpu.VMEM((nh, t_new, 1), F32), pltpu.VMEM((nh, page), F32),
                            pltpu.VMEM((nh, t_new, 1), F32), pltpu.VMEM((nh, t_new, 1), F32),
                            pltpu.VMEM((nh, t_new, dh), F32)]),
        out_shape=jax.ShapeDtypeStruct((db, nh, t_new, dh), F32),
        compiler_params=_params(("arbitrary", "arbitrary"), 52),
        name="fox_sample",
    )(page_table, q, k_new_t, v_new_t, lf_new_t, *[a for a in caches for _ in range(n_pp)])


def _split_cols(w, sizes):
    out, a = [], 0
    for s in sizes:
        out.append(w[:, a:a + s])
        a += s
    return out


def _ab_weights(w_in, w_a2, b_a2):
    aw = len(A_BRANCHES) * A_BW
    aq, ak, av, bq, bk, bv, br, ba = _split_cols(w_in, [aw, aw, aw, B_QK, B_QK, B_OUT, B_OUT, B_GATE_RANK])
    w = {"q": aq.astype(BF16), "b": jnp.concatenate([bq, bk, bv, br], axis=1).astype(BF16),
         "ba": ba.astype(BF16), "a2": w_a2.astype(BF16), "ba2": b_a2.reshape(1, -1)}
    for g in range(len(A_BRANCHES)):
        cols = slice(g * A_BW, (g + 1) * A_BW)
        w[f"kv{g}"] = jnp.concatenate([ak[:, cols], av[:, cols]], axis=1).astype(BF16)
    return w


def _fox_weights(w_in, b_f):
    cw = C_HEADS * HEAD_DIM
    q, k, v, f = _split_cols(w_in, [cw, cw, cw, C_HEADS])
    nparts = 3
    head = jnp.arange(C_HEADS)
    selq = jnp.zeros((nparts * C_HEADS, cw), F32)
    selk = jnp.zeros((cw, nparts * C_HEADS), F32)
    oneq = jnp.zeros((1, cw), F32)
    onek = jnp.zeros((cw, 1), F32)
    for j in range(nparts):
        selq = selq.at[j * C_HEADS + head, head * HEAD_DIM + nparts + j].set(1.0)
        selk = selk.at[head * HEAD_DIM + j, j * C_HEADS + head].set(-1.0)
        oneq = oneq.at[0, head * HEAD_DIM + j].set(1.0)
        onek = onek.at[head * HEAD_DIM + nparts + j, 0].set(1.0)
    return {"q": q.astype(BF16), "kt": k.T.astype(BF16), "vt": v.T.astype(BF16),
            "f": f.astype(BF16), "ft": f.T.astype(BF16), "bf": b_f.reshape(1, -1), "bft": b_f.reshape(-1, 1),
            "selq": selq.astype(BF16), "oneq": oneq, "selk": selk.astype(BF16), "onek": onek}


NEW_TOKEN_PAD = 128


def _pad_tokens(a):
    return jnp.pad(a, ((0, 0),) * (a.ndim - 1) + ((0, NEW_TOKEN_PAD - a.shape[-1]),))


def kernel(x_prompt, x_sample, cache_a0_kv, cache_a1_kv, cache_a2_kv, state_gla, cache_c_k, cache_c_v, cache_c_logf,
           page_table, c_prompt, c_sample, w_ada, b_ada, g_mix, g_mlp, w_in_ab, w_gla_a2, b_gla_a2, g_gla_out, w_out_ab,
           w_in_fox, b_fox_f, w_out_fox, w_up, w_down, g_final):
    nb, seq, d = x_prompt.shape
    db, t_new, _ = x_sample.shape
    depth = w_ada.shape[0]
    n_pool, page = cache_c_k.shape[1], cache_c_k.shape[2]
    tm_p, tm_s = 512, db * t_new

    ada = _adaln(jnp.concatenate([c_prompt, c_sample], axis=0), w_ada, b_ada)

    def terms(layer):
        tp = [ada[layer, :nb, i * d:(i + 1) * d][:, None, :] for i in range(6)]
        ts = [jnp.repeat(ada[layer, nb:, i * d:(i + 1) * d], t_new, axis=0)[None] for i in range(6)]
        return tp, ts

    xp = x_prompt
    xs = x_sample.reshape(1, db * t_new, d)
    a_caches = (cache_a0_kv, cache_a1_kv, cache_a2_kv)
    a_p, a_s = [[], [], []], [[], [], []]
    gla_p, gla_s = [], []
    ck_p, cv_p, cf_p, ck_s, cv_s, cf_s = [], [], [], [], [], []
    g_final2 = g_final.reshape(1, d)

    for layer in range(depth):
        tp, ts = terms(layer)
        g1 = g_mix[layer].reshape(1, d)
        g2 = g_mlp[layer].reshape(1, d)
        last = layer == depth - 1
        if layer % 2 == 0:
            e = layer // 2
            w = _ab_weights(w_in_ab[e], w_gla_a2[e], b_gla_a2[e])
            w_out_a = w_out_ab[e, :A_BW].astype(BF16)
            w_out_b = w_out_ab[e, A_BW:].astype(BF16)
            g_bo = g_gla_out[e].reshape(1, B_OUT)

            q, qv1, qv2, kv0, kvv1, kv2, kvv2, bproj, log_a = _ab_in(xp, g1, tp[0], tp[1], w, tm_p)
            branch = [_dil_prompt(q, kv0, 0, q_col=0), _dil_prompt(qv1, kvv1, 1, q_col=0),
                      _dil_prompt(qv2, kvv2, 2, q_col=0)]
            o_b, st = _gla(bproj, log_a, jnp.zeros((nb, B_DV, B_QK), F32), g_bo, chunk=128, sub=8)
            mix = [o for o, _ in branch] + [l for _, l in branch]
            xp = _proj_res([o_b], [w_out_a, w_out_b], xp, tp[2], tm_p, mix=mix)
            kvs = (kv0, kvv1.reshape(nb, seq, 2 * A_BW), kv2)
            for g, (win, _) in enumerate(A_BRANCHES):
                keep = min(win, seq)
                src = kvv1[:, (seq - keep) // A_BRANCHES[1][1]:] if g == 1 else kvs[g][:, seq - keep:]
                a_p[g].append(src.reshape(nb, keep, 2, A_HEADS, HEAD_DIM))
            gla_p.append(st.reshape(nb, B_DV, B_HEADS, B_DK).transpose(0, 2, 3, 1))

            q, _, _, kv0, kvv1, kv2, _, bproj, log_a = _ab_in(xs, g1, ts[0], ts[1], w, tm_s)
            kv1 = kvv1.reshape(kv0.shape)
            q_s = q.reshape(db, t_new, len(A_BRANCHES), A_HEADS, HEAD_DIM).transpose(0, 2, 3, 1, 4)
            new_t = [a.reshape(db, t_new, 2, A_HEADS, HEAD_DIM).transpose(0, 2, 3, 4, 1) for a in (kv0, kv1, kv2)]
            bufs_t = [c[e].transpose(0, 2, 3, 4, 1) for c in a_caches]
            o_a = _dil_sample(q_s, [_pad_tokens(a) for a in new_t], bufs_t)
            o_a = o_a.transpose(0, 2, 1, 3).reshape(1, db * t_new, A_BW).astype(BF16)
            s0_t = state_gla[e].transpose(0, 3, 1, 2).reshape(db, B_DV, B_QK)
            o_b, st = _gla(bproj.reshape(db, t_new, -1), log_a.reshape(db, t_new, B_QK), s0_t, g_bo,
                           chunk=t_new, sub=16)
            xs = _proj_res([o_a, o_b.reshape(1, db * t_new, B_OUT)], [w_out_a, w_out_b], xs, ts[2], tm_s)
            for g, (win, _) in enumerate(A_BRANCHES):
                full = jnp.concatenate([bufs_t[g], new_t[g]], axis=-1)
                keep = min(win, full.shape[-1])
                a_s[g].append(full[..., full.shape[-1] - keep:].transpose(0, 4, 1, 2, 3))
            gla_s.append(st.reshape(db, B_DV, B_HEADS, B_DK).transpose(0, 2, 3, 1))
        else:
            o_idx = layer // 2
            w = _fox_weights(w_in_fox[o_idx], b_fox_f[o_idx])
            w_out = w_out_fox[o_idx].astype(BF16)
            cw = C_HEADS * HEAD_DIM

            q_aug, kt, vt, kt_aug, vt_aug, lft = _fox_in(xp, g1, tp[0], tp[1], w, tm_p)
            o = _fox_prompt(q_aug, kt_aug, vt_aug)
            xp = _proj_res([o], [w_out], xp, tp[2], tm_p)
            ck_p.append(kt.reshape(nb, C_HEADS, HEAD_DIM, seq).transpose(0, 3, 1, 2))
            cv_p.append(vt.reshape(nb, C_HEADS, HEAD_DIM, seq).transpose(0, 3, 1, 2))
            cf_p.append(lft.transpose(0, 2, 1))

            q_aug, kt, vt, _, _, lft = _fox_in(xs, g1, ts[0], ts[1], w, tm_s)
            q_s = q_aug.reshape(db, t_new, C_HEADS, 2 * HEAD_DIM)[..., :HEAD_DIM].transpose(0, 2, 1, 3)
            k_new_t = kt.reshape(C_HEADS, HEAD_DIM, db, t_new).transpose(2, 0, 1, 3)
            v_new_t = vt.reshape(C_HEADS, HEAD_DIM, db, t_new).transpose(2, 0, 1, 3)
            lf_new_t = lft.reshape(C_HEADS, db, t_new).transpose(1, 0, 2)
            o = _fox_sample(q_s, _pad_tokens(k_new_t), _pad_tokens(v_new_t), _pad_tokens(lf_new_t),
                            cache_c_k.transpose(0, 1, 3, 4, 2).reshape(-1, C_HEADS, HEAD_DIM, page),
                            cache_c_v.transpose(0, 1, 3, 4, 2).reshape(-1, C_HEADS, HEAD_DIM, page),
                            cache_c_logf.transpose(0, 1, 3, 2).reshape(-1, C_HEADS, page), page_table, o_idx * n_pool)
            o = o.transpose(0, 2, 1, 3).reshape(1, db * t_new, cw).astype(BF16)
            xs = _proj_res([o], [w_out], xs, ts[2], tm_s)
            ck_s.append(k_new_t.transpose(0, 3, 1, 2))
            cv_s.append(v_new_t.transpose(0, 3, 1, 2))
            cf_s.append(lf_new_t.transpose(0, 2, 1))

        w_up_b = w_up[layer].astype(BF16)
        w_down_b = w_down[layer].astype(BF16)
        xp = _mlp(xp, g2, tp[3], tp[4], tp[5], w_up_b, w_down_b, g_final2, tm_p, last)
        xs = _mlp(xs, g2, ts[3], ts[4], ts[5], w_up_b, w_down_b, g_final2, tm_s, last)

    stack = lambda parts: jnp.stack(parts, axis=0)
    return (xp, xs.reshape(db, t_new, d),
            stack(a_p[0]), stack(a_p[1]), stack(a_p[2]), stack(gla_p),
            stack(ck_p), stack(cv_p), stack(cf_p),
            stack(a_s[0]), stack(a_s[1]), stack(a_s[2]), stack(gla_s),
            stack(ck_s), stack(cv_s), stack(cf_s))
```
